```python
import math
import jax, jax.numpy as jnp
from jax import lax
import numpy as np

D_MODEL = 2048
BATCH = 4
SEQ = 2048
DEPTH = 1
DEC_BATCH = 4
DEC_SEQ = 4096
PAST_LEN = 128

N_MEM = 256
D_MIX = D_MODEL
D_DELTA = D_MIX // 2
D_FNET = D_MIX - D_DELTA
DN_HEADS = 8
DN_HEAD_DIM = D_DELTA // DN_HEADS
FN_GROUPS = 4
FN_GROUP_DIM = D_FNET // FN_GROUPS
CONV_WIDTH = 5
CHUNK = 64
XA_HEADS = 4
XA_HEAD_DIM = D_MODEL // XA_HEADS
D_FF = 5504
D_DN_IN = 4 * D_DELTA + 4 * DN_HEADS
D_IN = D_DN_IN + D_FNET
EPS = 1e-6

kernel_name = 'hybrid_deltanet_fnet_macaron_encoder'


def rms_norm(x, g):
    xf = x.astype(jnp.float32)
    y = xf * lax.rsqrt(jnp.mean(xf * xf, axis=-1, keepdims=True) + EPS)
    return (y * g.astype(jnp.float32)).astype(x.dtype)


def l2_normalize(t):
    return t * lax.rsqrt(jnp.sum(t * t, axis=-1, keepdims=True) + EPS)


def swiglu(h, w_gate, w_up, w_down):
    return (jax.nn.silu(h @ w_gate) * (h @ w_up)) @ w_down


def centred_short_conv(u, w):
    c = u.shape[-1]
    y = lax.conv_general_dilated(
        u, w[:, None, :].astype(u.dtype), window_strides=(1,),
        padding=[(CONV_WIDTH // 2, CONV_WIDTH // 2)],
        dimension_numbers=('NWC', 'WIO', 'NWC'), feature_group_count=c)
    return jax.nn.silu(y)


def gated_delta_rule_chunked(q, k, v, beta, g):
    b, h, s, dk = q.shape
    dv = v.shape[-1]
    n = s // CHUNK
    q = q.reshape(b, h, n, CHUNK, dk)
    k = k.reshape(b, h, n, CHUNK, dk)
    v = v.reshape(b, h, n, CHUNK, dv)
    beta = beta.reshape(b, h, n, CHUNK)
    cum_g = jnp.cumsum(g.reshape(b, h, n, CHUNK), axis=-1)
    incl = jnp.tril(jnp.ones((CHUNK, CHUNK), dtype=bool))
    strict = jnp.tril(jnp.ones((CHUNK, CHUNK), dtype=bool), k=-1)
    diff = cum_g[..., :, None] - cum_g[..., None, :]
    decay = jnp.where(incl, jnp.exp(jnp.where(incl, diff, 0.0)), 0.0)
    kb = k * beta[..., None]
    a_mat = jnp.where(strict, jnp.einsum('bhnid,bhnjd->bhnij', kb, k) * decay, 0.0)
    lhs = a_mat + jnp.eye(CHUNK, dtype=jnp.float32)
    rhs = jnp.concatenate([v * beta[..., None], kb * jnp.exp(cum_g)[..., None]], axis=-1)
    sol = lax.linalg.triangular_solve(lhs, rhs, left_side=True, lower=True, unit_diagonal=True)
    u_c, w_c = sol[..., :dv], sol[..., dv:]
    attn = jnp.einsum('bhnid,bhnjd->bhnij', q, k) * decay
    q_dec = q * jnp.exp(cum_g)[..., None]
    k_dec = k * jnp.exp(cum_g[..., -1:] - cum_g)[..., None]
    last = jnp.exp(cum_g[..., -1])

    def step(state, xs):
        qd, wc, uc, at, kd, gl = xs
        v_new = uc - jnp.einsum('bhik,bhkv->bhiv', wc, state)
        o = jnp.einsum('bhik,bhkv->bhiv', qd, state) + jnp.einsum('bhij,bhjv->bhiv', at, v_new)
        state = state * gl[..., None, None] + jnp.einsum('bhik,bhiv->bhkv', kd, v_new)
        return state, o

    xs = tuple(jnp.moveaxis(t, 2, 0) for t in (q_dec, w_c, u_c, attn, k_dec, last))
    s0 = jnp.zeros((b, h, dk, dv), jnp.float32)
    _, o = lax.scan(step, s0, xs)
    return jnp.moveaxis(o, 0, 2).reshape(b, h, s, dv)


def deltanet_mixer(p_dn, conv_w, a_log, dt_bias, head_norm):
    b, s, _ = p_dn.shape
    f32 = jnp.float32
    qkv = centred_short_conv(p_dn[..., :3 * D_DELTA], conv_w)
    z = p_dn[..., 3 * D_DELTA:4 * D_DELTA]
    off = 4 * D_DELTA
    beta_logit = p_dn[..., off:off + 2 * DN_HEADS].astype(f32).reshape(b, s, 2, DN_HEADS)
    a_logit = p_dn[..., off + 2 * DN_HEADS:off + 4 * DN_HEADS].astype(f32).reshape(b, s, 2, DN_HEADS)
    beta = jnp.transpose(jax.nn.sigmoid(beta_logit), (2, 0, 3, 1))
    g = -jnp.exp(a_log.astype(f32))[:, None, :, None] * jnp.transpose(
        jax.nn.softplus(a_logit + dt_bias.astype(f32)), (2, 0, 3, 1))

    def heads(t):
        return t.astype(f32).reshape(b, s, DN_HEADS, DN_HEAD_DIM).transpose(0, 2, 1, 3)

    q = l2_normalize(heads(qkv[..., :D_DELTA])) * (DN_HEAD_DIM ** -0.5)
    k = l2_normalize(heads(qkv[..., D_DELTA:2 * D_DELTA]))
    v = heads(qkv[..., 2 * D_DELTA:])
    o_fwd = gated_delta_rule_chunked(q, k, v, beta[0], g[0])

    def rev(t):
        return jnp.flip(t, axis=2)

    o_bwd = rev(gated_delta_rule_chunked(rev(q), rev(k), rev(v), rev(beta[1]), rev(g[1])))
    o = (o_fwd + o_bwd).transpose(0, 2, 1, 3)
    o = rms_norm(o, head_norm) * jax.nn.silu(z.astype(f32)).reshape(b, s, DN_HEADS, DN_HEAD_DIM)
    return o.reshape(b, s, D_DELTA).astype(p_dn.dtype)


def fourier_mixer(p_fn):
    b, s, _ = p_fn.shape
    u = p_fn.astype(jnp.float32).reshape(b, s, FN_GROUPS, FN_GROUP_DIM)
    y = jnp.real(jnp.fft.fft2(u, axes=(1, 3), norm='ortho'))
    return y.reshape(b, s, D_FNET).astype(p_fn.dtype)


def cross_attention(h, m, w_q, w_kv, w_o):
    b, s, _ = h.shape
    n_mem = m.shape[1]
    q = (h @ w_q).reshape(b, s, XA_HEADS, XA_HEAD_DIM)
    kv = (m @ w_kv).reshape(b, n_mem, 2, XA_HEADS, XA_HEAD_DIM)
    k, v = kv[:, :, 0], kv[:, :, 1]
    scores = jnp.einsum('bshd,bmhd->bhsm', q, k).astype(jnp.float32) * (XA_HEAD_DIM ** -0.5)
    probs = jax.nn.softmax(scores, axis=-1).astype(h.dtype)
    o = jnp.einsum('bhsm,bmhd->bshd', probs, v).reshape(b, s, D_MODEL)
    return o @ w_o


def encoder_layer(x, mem, ffn1_norm, ffn1_w_gate, ffn1_w_up, ffn1_w_down, mix_norm, w_in,
                  conv_w, a_log, dt_bias, dn_head_norm, w_out, xattn_norm, mem_norm,
                  xattn_w_q, xattn_w_kv, xattn_w_o, ffn2_norm, ffn2_w_gate, ffn2_w_up, ffn2_w_down):
    x = x + 0.5 * swiglu(rms_norm(x, ffn1_norm), ffn1_w_gate, ffn1_w_up, ffn1_w_down)
    p = rms_norm(x, mix_norm) @ w_in
    o_dn = deltanet_mixer(p[..., :D_DN_IN], conv_w, a_log, dt_bias, dn_head_norm)
    o_fn = fourier_mixer(p[..., D_DN_IN:])
    x = x + jnp.concatenate([o_dn, o_fn], axis=-1) @ w_out
    x = x + cross_attention(rms_norm(x, xattn_norm), rms_norm(mem, mem_norm),
                            xattn_w_q, xattn_w_kv, xattn_w_o)
    x = x + 0.5 * swiglu(rms_norm(x, ffn2_norm), ffn2_w_gate, ffn2_w_up, ffn2_w_down)
    return x


def encoder_trunk(x, mem, layer_weights, final_norm):
    for l in range(DEPTH):
        x = encoder_layer(x, mem, *[w[l] for w in layer_weights])
    return rms_norm(x, final_norm)


def setup_inputs(seed: int = 0) -> dict:
    key = jax.random.key(seed)
    ks = jax.random.split(key, 25)
    f32 = jnp.float32

    def nrm(k, shape, fan_in):
        return jax.random.normal(k, shape, f32) * (fan_in ** -0.5)

    def gain(k, shape):
        return 1.0 + 0.02 * jax.random.normal(k, shape, f32)

    dt = jnp.exp(jax.random.uniform(ks[12], (DEPTH, 2, DN_HEADS), f32,
                                    math.log(1e-3), math.log(1e-1)))
    return {
        'x_prompt': jax.random.normal(ks[0], (BATCH, SEQ, D_MODEL), f32),
        'x_sample': jax.random.normal(ks[1], (DEC_BATCH, DEC_SEQ, D_MODEL), f32),
        'mem_prompt': jax.random.normal(ks[2], (BATCH, N_MEM, D_MODEL), f32),
        'mem_sample': jax.random.normal(ks[3], (DEC_BATCH, N_MEM, D_MODEL), f32),
        'ffn1_norm': gain(ks[4], (DEPTH, D_MODEL)),
        'ffn1_w_gate': nrm(ks[5], (DEPTH, D_MODEL, D_FF), D_MODEL),
        'ffn1_w_up': nrm(ks[6], (DEPTH, D_MODEL, D_FF), D_MODEL),
        'ffn1_w_down': nrm(ks[7], (DEPTH, D_FF, D_MODEL), D_FF),
        'mix_norm': gain(ks[8], (DEPTH, D_MODEL)),
        'w_in': nrm(ks[9], (DEPTH, D_MODEL, D_IN), D_MODEL),
        'conv_w': nrm(ks[10], (DEPTH, CONV_WIDTH, 3 * D_DELTA), CONV_WIDTH),
        'a_log': jnp.log(jax.random.uniform(ks[11], (DEPTH, 2, DN_HEADS), f32, 1.0, 16.0)),
        'dt_bias': dt + jnp.log(-jnp.expm1(-dt)),
        'dn_head_norm': gain(ks[13], (DEPTH, DN_HEAD_DIM)),
        'w_out': nrm(ks[14], (DEPTH, D_MIX, D_MODEL), D_MIX),
        'xattn_norm': gain(ks[15], (DEPTH, D_MODEL)),
        'mem_norm': gain(ks[16], (DEPTH, D_MODEL)),
        'xattn_w_q': nrm(ks[17], (DEPTH, D_MODEL, D_MODEL), D_MODEL),
        'xattn_w_kv': nrm(ks[18], (DEPTH, D_MODEL, 2 * D_MODEL), D_MODEL),
        'xattn_w_o': nrm(ks[19], (DEPTH, D_MODEL, D_MODEL), D_MODEL),
        'ffn2_norm': gain(ks[20], (DEPTH, D_MODEL)),
        'ffn2_w_gate': nrm(ks[21], (DEPTH, D_MODEL, D_FF), D_MODEL),
        'ffn2_w_up': nrm(ks[22], (DEPTH, D_MODEL, D_FF), D_MODEL),
        'ffn2_w_down': nrm(ks[23], (DEPTH, D_FF, D_MODEL), D_FF),
        'final_norm': gain(ks[24], (D_MODEL,)),
    }


def reference(x_prompt, x_sample, mem_prompt, mem_sample, ffn1_norm, ffn1_w_gate, ffn1_w_up,
              ffn1_w_down, mix_norm, w_in, conv_w, a_log, dt_bias, dn_head_norm, w_out,
              xattn_norm, mem_norm, xattn_w_q, xattn_w_kv, xattn_w_o, ffn2_norm, ffn2_w_gate,
              ffn2_w_up, ffn2_w_down, final_norm):
    layer_weights = (ffn1_norm, ffn1_w_gate, ffn1_w_up, ffn1_w_down, mix_norm, w_in, conv_w,
                     a_log, dt_bias, dn_head_norm, w_out, xattn_norm, mem_norm, xattn_w_q,
                     xattn_w_kv, xattn_w_o, ffn2_norm, ffn2_w_gate, ffn2_w_up, ffn2_w_down)
    y_prompt = encoder_trunk(x_prompt, mem_prompt, layer_weights, final_norm)
    y_sample = encoder_trunk(x_sample, mem_sample, layer_weights, final_norm)
    return (y_prompt, y_sample)
```

```python
import functools
import math

import jax
import jax.numpy as jnp
from jax import lax
from jax.experimental import pallas as pl
from jax.experimental.pallas import tpu as pltpu

F32 = jnp.float32
BF16 = jnp.bfloat16
EPS = 1e-6

DN_HEADS = 8
DN_HEAD_DIM = 128
FN_GROUPS = 4
CONV_WIDTH = 5
CHUNK = 64
XA_HEADS = 4
N_MEM = 256

LANES = 128
SUBLANES = 8
VMEM_LIMIT = 56 * 1024 * 1024

NT_DIMS = (((1,), (1,)), ((), ()))
TN_DIMS = (((0,), (0,)), ((), ()))


def _params(*sem):
    return pltpu.CompilerParams(dimension_semantics=sem, vmem_limit_bytes=VMEM_LIMIT)


def _dot(a, b):
    return jnp.dot(a, b, preferred_element_type=F32)


def _rms(x, g):
    ms = jnp.mean(x * x, axis=-1, keepdims=True)
    return x * lax.rsqrt(ms + EPS) * g


def _sigmoid(x):
    return 1.0 / (1.0 + jnp.exp(-x))


def _ffn_body(x_ref, g_ref, wg_ref, wu_ref, wd_ref, fin_ref, o_ref, h_ref, acc_ref, *, final):
    j = pl.program_id(1)

    @pl.when(j == 0)
    def _():
        h_ref[...] = _rms(x_ref[...], g_ref[...]).astype(BF16)
        acc_ref[...] = jnp.zeros_like(acc_ref)

    h = h_ref[...]
    a = _dot(h, wg_ref[...])
    u = _dot(h, wu_ref[...])
    act = (a * _sigmoid(a) * u).astype(BF16)
    acc_ref[...] += _dot(act, wd_ref[...])

    @pl.when(j == pl.num_programs(1) - 1)
    def _():
        y = x_ref[...] + 0.5 * acc_ref[...]
        if final:
            y = _rms(y, fin_ref[...])
        o_ref[...] = y


def _ffn(x, g, wg, wu, wd, fin, *, final, tm=512, tf=512):
    t, d = x.shape
    fp = wg.shape[1]
    return pl.pallas_call(
        functools.partial(_ffn_body, final=final),
        grid=(t // tm, fp // tf),
        in_specs=[
            pl.BlockSpec((tm, d), lambda i, j: (i, 0)),
            pl.BlockSpec((1, d), lambda i, j: (0, 0)),
            pl.BlockSpec((d, tf), lambda i, j: (0, j)),
            pl.BlockSpec((d, tf), lambda i, j: (0, j)),
            pl.BlockSpec((tf, d), lambda i, j: (j, 0)),
            pl.BlockSpec((1, d), lambda i, j: (0, 0)),
        ],
        out_specs=pl.BlockSpec((tm, d), lambda i, j: (i, 0)),
        out_shape=jax.ShapeDtypeStruct((t, d), F32),
        scratch_shapes=[pltpu.VMEM((tm, d), BF16), pltpu.VMEM((tm, d), F32)],
        compiler_params=_params("parallel", "arbitrary"),
        name="ffn_final" if final else "ffn",
    )(x, g, wg, wu, wd, fin)


def _inproj_body(x_ref, g_ref, w_ref, wgt_ref, alog_ref, dt_ref, p_ref, gt_ref, h_ref):
    j = pl.program_id(1)

    @pl.when(j == 0)
    def _():
        h = _rms(x_ref[...], g_ref[...]).astype(BF16)
        h_ref[...] = h
        gt = lax.dot_general(wgt_ref[...], h, NT_DIMS, preferred_element_type=F32)
        kind = lax.broadcasted_iota(jnp.int32, gt.shape, 0) & (SUBLANES - 1)
        xs = gt + dt_ref[...]
        softplus = jnp.maximum(xs, 0.0) + jnp.log1p(jnp.exp(-jnp.abs(xs)))
        log_decay = -jnp.exp(alog_ref[...]) * softplus
        gt_ref[...] = jnp.where(kind < 2, _sigmoid(gt), jnp.where(kind < 4, log_decay, 0.0))

    p_ref[...] = _dot(h_ref[...], w_ref[...]).astype(BF16)


def _inproj(x, g, w, wgt, alog, dt, *, tm=512, tn=1024):
    t, d = x.shape
    n = w.shape[1]
    rows = wgt.shape[0]
    return pl.pallas_call(
        _inproj_body,
        grid=(t // tm, n // tn),
        in_specs=[
            pl.BlockSpec((tm, d), lambda i, j: (i, 0)),
            pl.BlockSpec((1, d), lambda i, j: (0, 0)),
            pl.BlockSpec((d, tn), lambda i, j: (0, j)),
            pl.BlockSpec((rows, d), lambda i, j: (0, 0)),
            pl.BlockSpec((rows, tm), lambda i, j: (0, 0)),
            pl.BlockSpec((rows, tm), lambda i, j: (0, 0)),
        ],
        out_specs=[
            pl.BlockSpec((tm, tn), lambda i, j: (i, j)),
            pl.BlockSpec((rows, tm), lambda i, j: (0, i)),
        ],
        out_shape=[jax.ShapeDtypeStruct((t, n), BF16), jax.ShapeDtypeStruct((rows, t), F32)],
        scratch_shapes=[pltpu.VMEM((tm, d), BF16)],
        compiler_params=_params("parallel", "arbitrary"),
        name="inproj",
    )(x, g, w, wgt, alog, dt)


CONV_ROWS = 256
HALO = SUBLANES
PAIR = 2 * CHUNK


def _split2(x):
    hi = x.astype(BF16)
    lo = (x - hi.astype(F32)).astype(BF16)
    return hi, lo


def _deltanet_body(q_ref, k_ref, v_ref, z_ref, cwq_ref, cwk_ref, cwv_ref, gp_ref, hn_ref, o_ref,
                   upad, qs, ks, vs, u_scr, wqf, wqb, kdf, kdb, at_scr, el_scr, sf, sb, of, ob, bd1,
                   *, seq):
    nc = seq // CHUNK
    c = CHUNK

    def conv_stream(src_ref, cw_ref, dst_ref, scale):
        zeros = jnp.zeros((HALO, LANES), F32)
        upad[0:HALO, :] = zeros
        upad[seq + HALO:seq + 2 * HALO, :] = zeros

        def fill(r, _):
            r0 = pl.multiple_of(r * CONV_ROWS, CONV_ROWS)
            upad[pl.ds(r0 + HALO, CONV_ROWS), :] = src_ref[pl.ds(r0, CONV_ROWS), :].astype(F32)
            return 0

        lax.fori_loop(0, seq // CONV_ROWS, fill, 0)
        cw = cw_ref[...]
        n = CONV_ROWS + 2 * HALO

        def block(r, _):
            r0 = pl.multiple_of(r * CONV_ROWS, CONV_ROWS)
            xb = upad[pl.ds(r0, n), :]
            acc = None
            for tap in range(CONV_WIDTH):
                shift = (CONV_WIDTH // 2 - tap) % n
                term = cw[tap:tap + 1, :] * (pltpu.roll(xb, shift, 0) if shift else xb)
                acc = term if acc is None else acc + term
            y = acc[HALO:HALO + CONV_ROWS, :]
            y = y * _sigmoid(y)
            if scale is not None:
                y = y * lax.rsqrt(jnp.sum(y * y, axis=-1, keepdims=True) + EPS) * scale
            dst_ref[pl.ds(r0, CONV_ROWS), :] = y
            return 0

        lax.fori_loop(0, seq // CONV_ROWS, block, 0)

    conv_stream(q_ref, cwq_ref, qs, DN_HEAD_DIM ** -0.5)
    conv_stream(k_ref, cwk_ref, ks, 1.0)
    conv_stream(v_ref, cwv_ref, vs, None)

    ii = lax.broadcasted_iota(jnp.int32, (c, PAIR), 0)
    ll = lax.broadcasted_iota(jnp.int32, (c, PAIR), 1)
    jj = ll & (c - 1)
    is_f = ll < c
    ahead = jnp.where(is_f, ii - jj, jj - ii)
    incl = ahead >= 0
    strict = ahead > 0
    eye = ii == jj
    xor = ii ^ jj

    rr = lax.broadcasted_iota(jnp.int32, (2 * PAIR, 2 * PAIR), 0)
    nn = lax.broadcasted_iota(jnp.int32, (2 * PAIR, 2 * PAIR), 1)
    bd1[...] = jnp.where(((rr & (PAIR - 1)) >> (c.bit_length() - 1)) == (nn >> (PAIR.bit_length() - 1)),
                         1.0, 0.0).astype(BF16)

    def blockdiag(xp):
        top = jnp.where(is_f, xp, 0.0)
        bot = jnp.where(is_f, 0.0, xp)
        return jnp.concatenate([top, bot], axis=0).astype(BF16)

    def prep(ci, _):
        r0 = pl.multiple_of(ci * c, c)
        kf = ks[pl.ds(r0, c), :]
        qf = qs[pl.ds(r0, c), :]
        kb = kf.astype(BF16)
        qb = qf.astype(BF16)
        vb = vs[pl.ds(r0, c), :].astype(BF16)
        gram = lax.dot_general(jnp.concatenate([kb, qb], axis=0), jnp.concatenate([kb, kb], axis=0),
                               NT_DIMS, preferred_element_type=F32)
        tile = gp_ref[ci]
        brow = tile[0:1, :]
        grow = tile[1:2, :]
        x = jnp.concatenate([jnp.where(incl, grow, 0.0), jnp.where(eye, brow, 0.0)], axis=0)
        hi, lo = _split2(x)
        cb = _dot(jnp.concatenate([hi, lo], axis=1), bd1[...])
        cf_b = cb[0:c, 0:PAIR]
        cb_b = cb[0:c, PAIR:2 * PAIR]
        cpair = jnp.where(is_f, cf_b, cb_b)
        bpair = jnp.where(is_f, cb[c:2 * c, 0:PAIR], cb[c:2 * c, PAIR:2 * PAIR])
        crow = jnp.sum(jnp.where(eye, cpair, 0.0), axis=0, keepdims=True)
        decay = jnp.where(incl, jnp.exp(jnp.where(incl, cpair - crow, 0.0)), 0.0)
        a = jnp.where(strict, bpair * gram[0:c, :] * decay, 0.0)
        at_scr[pl.ds(r0, c), :] = (gram[c:2 * c, :] * decay).astype(BF16)

        t = jnp.where(eye, 1.0, 0.0) - jnp.where(xor == 1, a, 0.0)
        for lvl in range(1, 6):
            e = jnp.where((xor >> lvl) == 1, a, 0.0)
            te = _dot(t.astype(BF16), blockdiag(e))
            t = t - _dot(te.astype(BF16), blockdiag(t))

        erow = jnp.exp(crow)
        zer = jnp.zeros((c, LANES), BF16)
        bdv = jnp.concatenate([jnp.concatenate([vb, zer], axis=1), jnp.concatenate([zer, vb], axis=1)], axis=0)
        bdk = jnp.concatenate([jnp.concatenate([kb, zer], axis=1), jnp.concatenate([zer, kb], axis=1)], axis=0)
        u_scr[pl.ds(r0, c), :] = _dot((t * brow).astype(BF16), bdv)
        w = _dot((t * (brow * erow)).astype(BF16), bdk)

        w0 = pl.multiple_of(ci * (2 * c), 2 * c)
        last_f = cf_b[c - 1:c, :]
        last_b = cb_b[0:1, :]
        wqf[pl.ds(w0, c), :] = w[:, 0:LANES].astype(BF16)
        wqf[pl.ds(w0 + c, c), :] = (qf * jnp.exp(cf_b)).astype(BF16)
        wqb[pl.ds(w0, c), :] = w[:, LANES:2 * LANES].astype(BF16)
        wqb[pl.ds(w0 + c, c), :] = (qf * jnp.exp(cb_b)).astype(BF16)
        kdf[pl.ds(r0, c), :] = (kf * jnp.exp(last_f - cf_b)).astype(BF16)
        kdb[pl.ds(r0, c), :] = (kf * jnp.exp(last_b - cb_b)).astype(BF16)
        el_scr[ci] = jnp.concatenate([jnp.exp(last_f), jnp.exp(last_b),
                                      jnp.zeros((SUBLANES - 2, LANES), F32)], axis=0)
        return 0

    lax.fori_loop(0, nc, prep, 0)

    sf[...] = jnp.zeros_like(sf)
    sb[...] = jnp.zeros_like(sb)
    lane = lax.broadcasted_iota(jnp.int32, (c, LANES), 1)
    lane_f = lane < c
    lane_b = lane >= c

    def one_dir(ci, wq_ref, kd_ref, s_ref, o_dst, lane0, elrow, keep):
        r0 = pl.multiple_of(ci * c, c)
        w0 = pl.multiple_of(ci * (2 * c), 2 * c)
        s = s_ref[...]
        r = _dot(wq_ref[pl.ds(w0, 2 * c), :], s.astype(BF16))
        vnew = u_scr[pl.ds(r0, c), lane0:lane0 + LANES] - r[0:c, :]
        vv = vnew.astype(BF16)
        attn = jnp.where(keep, at_scr[pl.ds(r0, c), :], jnp.zeros((c, LANES), BF16))
        o_dst[pl.ds(r0, c), :] = r[c:2 * c, :] + _dot(attn, jnp.concatenate([vv, vv], axis=0))
        el = el_scr[ci]
        s_ref[...] = s * el[elrow:elrow + 1, :] + lax.dot_general(
            kd_ref[pl.ds(r0, c), :], vv, TN_DIMS, preferred_element_type=F32)

    def step(i, _):
        one_dir(i, wqf, kdf, sf, of, 0, 0, lane_f)
        one_dir(nc - 1 - i, wqb, kdb, sb, ob, LANES, 1, lane_b)
        return 0

    lax.fori_loop(0, nc, step, 0)

    hn = hn_ref[...]

    def finish(r, _):
        r0 = pl.multiple_of(r * CONV_ROWS, CONV_ROWS)
        o = of[pl.ds(r0, CONV_ROWS), :] + ob[pl.ds(r0, CONV_ROWS), :]
        zz = z_ref[pl.ds(r0, CONV_ROWS), :].astype(F32)
        o_ref[pl.ds(r0, CONV_ROWS), :] = (_rms(o, hn) * (zz * _sigmoid(zz))).astype(BF16)
        return 0

    lax.fori_loop(0, seq // CONV_ROWS, finish, 0)


def _deltanet(p, cw, gp, hn, *, batch, seq):
    nc = seq // CHUNK
    h = DN_HEADS
    col = lambda off: pl.BlockSpec((None, seq, LANES), lambda b, hd, off=off: (b, 0, off + hd))
    cws = lambda off: pl.BlockSpec((SUBLANES, LANES), lambda b, hd, off=off: (0, off + hd))
    return pl.pallas_call(
        functools.partial(_deltanet_body, seq=seq),
        grid=(batch, h),
        in_specs=[col(0), col(h), col(2 * h), col(3 * h), cws(0), cws(h), cws(2 * h),
                  pl.BlockSpec((nc, SUBLANES, LANES), lambda b, hd: (b, hd, 0)),
                  pl.BlockSpec((1, LANES), lambda b, hd: (0, 0))],
        out_specs=pl.BlockSpec((None, seq, LANES), lambda b, hd: (b, 0, hd)),
        out_shape=jax.ShapeDtypeStruct((batch, seq, h * LANES), BF16),
        scratch_shapes=[
            pltpu.VMEM((seq + 2 * HALO, LANES), F32),
            pltpu.VMEM((seq, LANES), F32),
            pltpu.VMEM((seq, LANES), F32),
            pltpu.VMEM((seq, LANES), F32),
            pltpu.VMEM((seq, 2 * LANES), F32),
            pltpu.VMEM((2 * seq, LANES), BF16),
            pltpu.VMEM((2 * seq, LANES), BF16),
            pltpu.VMEM((seq, LANES), BF16),
            pltpu.VMEM((seq, LANES), BF16),
            pltpu.VMEM((seq, LANES), BF16),
            pltpu.VMEM((nc, SUBLANES, LANES), F32),
            pltpu.VMEM((LANES, LANES), F32),
            pltpu.VMEM((LANES, LANES), F32),
            pltpu.VMEM((seq, LANES), F32),
            pltpu.VMEM((seq, LANES), F32),
            pltpu.VMEM((2 * PAIR, 2 * PAIR), BF16),
        ],
        compiler_params=_params("parallel", "parallel"),
        name="deltanet",
    )(p, p, p, p, cw, cw, cw, gp, hn)


TW_ROWS = 64


def _twiddle_body(ac_ref, as_ref, bc_ref, bs_ref, c_ref, s_ref):
    ac = ac_ref[...]
    sn = as_ref[...]
    bc = bc_ref[...]
    bs = bs_ref[...]
    c_ref[...] = (ac * bc - sn * bs).astype(BF16)
    s_ref[...] = (-(sn * bc + ac * bs)).astype(BF16)


def _twiddle(seq):
    nblk = seq // TW_ROWS
    kk = jnp.arange(seq, dtype=jnp.int32)[None, :]
    step = 2.0 * math.pi / seq
    ang_a = ((jnp.arange(TW_ROWS, dtype=jnp.int32)[:, None] * kk) % seq).astype(F32) * step
    ang_b = ((jnp.arange(nblk, dtype=jnp.int32)[:, None] * TW_ROWS * kk) % seq).astype(F32) * step
    tab = pl.BlockSpec((TW_ROWS, seq), lambda i: (0, 0))
    row = pl.BlockSpec((None, 1, seq), lambda i: (i, 0, 0))
    out = pl.BlockSpec((TW_ROWS, seq), lambda i: (i, 0))
    return pl.pallas_call(
        _twiddle_body,
        grid=(nblk,),
        in_specs=[tab, tab, row, row],
        out_specs=[out, out],
        out_shape=[jax.ShapeDtypeStruct((seq, seq), BF16)] * 2,
        compiler_params=_params("parallel"),
        name="twiddle",
    )(jnp.cos(ang_a), jnp.sin(ang_a), jnp.cos(ang_b)[:, None, :], jnp.sin(ang_b)[:, None, :])


def _chandft_body(u_ref, w_ref, rc_ref, rs_ref):
    gd = w_ref.shape[0]
    w = w_ref[...]
    for g in range(FN_GROUPS):
        y = _dot(u_ref[:, g * gd:(g + 1) * gd], w)
        rc_ref[:, g * gd:(g + 1) * gd] = y[:, 0:gd].astype(BF16)
        rs_ref[:, g * gd:(g + 1) * gd] = y[:, gd:2 * gd].astype(BF16)


def _chandft(p, wch, *, col_block, tm=512):
    t = p.shape[0]
    dfn = FN_GROUPS * wch.shape[0]
    out = pl.BlockSpec((tm, dfn), lambda i: (i, 0))
    return pl.pallas_call(
        _chandft_body,
        grid=(t // tm,),
        in_specs=[pl.BlockSpec((tm, dfn), lambda i: (i, col_block)),
                  pl.BlockSpec(wch.shape, lambda i: (0, 0))],
        out_specs=[out, out],
        out_shape=[jax.ShapeDtypeStruct((t, dfn), BF16)] * 2,
        compiler_params=_params("parallel"),
        name="chandft",
    )(p, wch)


def _seqdft_body(c_ref, s_ref, rc_ref, rs_ref, o_ref):
    o_ref[...] = (_dot(c_ref[...], rc_ref[...]) + _dot(s_ref[...], rs_ref[...])).astype(BF16)


def _seqdft(ctw, stw, rc, rs, *, batch, seq, tm=256):
    dfn = rc.shape[-1]
    tw = pl.BlockSpec((tm, seq), lambda b, i: (i, 0))
    rhs = pl.BlockSpec((None, seq, dfn), lambda b, i: (b, 0, 0))
    return pl.pallas_call(
        _seqdft_body,
        grid=(batch, seq // tm),
        in_specs=[tw, tw, rhs, rhs],
        out_specs=pl.BlockSpec((None, tm, dfn), lambda b, i: (b, i, 0)),
        out_shape=jax.ShapeDtypeStruct((batch, seq, dfn), BF16),
        compiler_params=_params("parallel", "arbitrary"),
        name="seqdft",
    )(ctw, stw, rc.reshape(batch, seq, dfn), rs.reshape(batch, seq, dfn))


def _outproj_body(x_ref, a_ref, b_ref, wa_ref, wb_ref, o_ref):
    o_ref[...] = x_ref[...] + _dot(a_ref[...], wa_ref[...]) + _dot(b_ref[...], wb_ref[...])


def _outproj(x, a, b, wa, wb, *, tm=512):
    t, d = x.shape
    da = a.shape[1]
    db = b.shape[1]
    return pl.pallas_call(
        _outproj_body,
        grid=(t // tm,),
        in_specs=[pl.BlockSpec((tm, d), lambda i: (i, 0)),
                  pl.BlockSpec((tm, da), lambda i: (i, 0)),
                  pl.BlockSpec((tm, db), lambda i: (i, 0)),
                  pl.BlockSpec((da, d), lambda i: (0, 0)),
                  pl.BlockSpec((db, d), lambda i: (0, 0))],
        out_specs=pl.BlockSpec((tm, d), lambda i: (i, 0)),
        out_shape=jax.ShapeDtypeStruct((t, d), F32),
        compiler_params=_params("parallel"),
        name="outproj",
    )(x, a, b, wa, wb)


def _kvproj_body(m_ref, g_ref, w_ref, o_ref, h_ref):
    @pl.when(pl.program_id(1) == 0)
    def _():
        h_ref[...] = _rms(m_ref[...], g_ref[...]).astype(BF16)

    o_ref[...] = _dot(h_ref[...], w_ref[...]).astype(BF16)


def _kvproj(mem, g, w, *, tn=1024):
    b, n, d = mem.shape
    nout = w.shape[1]
    return pl.pallas_call(
        _kvproj_body,
        grid=(b, nout // tn),
        in_specs=[pl.BlockSpec((None, n, d), lambda i, j: (i, 0, 0)),
                  pl.BlockSpec((1, d), lambda i, j: (0, 0)),
                  pl.BlockSpec((d, tn), lambda i, j: (0, j))],
        out_specs=pl.BlockSpec((None, n, tn), lambda i, j: (i, 0, j)),
        out_shape=jax.ShapeDtypeStruct((b, n, nout), BF16),
        scratch_shapes=[pltpu.VMEM((n, d), BF16)],
        compiler_params=_params("parallel", "arbitrary"),
        name="kvproj",
    )(mem, g, w)


def _xattn_body(x_ref, g_ref, wq_ref, kv_ref, wo_ref, o_ref, q_scr, a_scr):
    d = x_ref.shape[-1]
    hd = d // XA_HEADS
    x = x_ref[...]
    q_scr[...] = _dot(_rms(x, g_ref[...]).astype(BF16), wq_ref[...]).astype(BF16)
    for h in range(XA_HEADS):
        kh = kv_ref[:, h * hd:(h + 1) * hd]
        vh = kv_ref[:, d + h * hd:d + (h + 1) * hd]
        s = lax.dot_general(q_scr[:, h * hd:(h + 1) * hd], kh, NT_DIMS,
                            preferred_element_type=F32) * (hd ** -0.5)
        p = jnp.exp(s - jnp.max(s, axis=-1, keepdims=True))
        l = jnp.sum(p, axis=-1, keepdims=True)
        a_scr[:, h * hd:(h + 1) * hd] = (_dot(p.astype(BF16), vh) / l).astype(BF16)
    o_ref[...] = x + _dot(a_scr[...], wo_ref[...])


def _xattn(x, g, wq, kv, wo, *, batch, seq, tm=256):
    d = x.shape[-1]
    n = kv.shape[1]
    return pl.pallas_call(
        _xattn_body,
        grid=(batch, seq // tm),
        in_specs=[pl.BlockSpec((None, tm, d), lambda b, i: (b, i, 0)),
                  pl.BlockSpec((1, d), lambda b, i: (0, 0)),
                  pl.BlockSpec((d, d), lambda b, i: (0, 0)),
                  pl.BlockSpec((None, n, 2 * d), lambda b, i: (b, 0, 0)),
                  pl.BlockSpec((d, d), lambda b, i: (0, 0))],
        out_specs=pl.BlockSpec((None, tm, d), lambda b, i: (b, i, 0)),
        out_shape=jax.ShapeDtypeStruct((batch, seq, d), F32),
        scratch_shapes=[pltpu.VMEM((tm, d), BF16), pltpu.VMEM((tm, d), BF16)],
        compiler_params=_params("parallel", "arbitrary"),
        name="xattn",
    )(x.reshape(batch, seq, d), g, wq, kv, wo)


def _prep_weights(ffn1_norm, ffn1_w_gate, ffn1_w_up, ffn1_w_down, mix_norm, w_in, conv_w, a_log, dt_bias,
                  dn_head_norm, w_out, xattn_norm, mem_norm, xattn_w_q, xattn_w_kv, xattn_w_o, ffn2_norm,
                  ffn2_w_gate, ffn2_w_up, ffn2_w_down, final_norm, *, tm_inproj):
    d = w_in.shape[0]
    dd = DN_HEADS * DN_HEAD_DIM
    dff = ffn1_w_gate.shape[1]
    fpad = -dff % 512

    def ffn_w(wg, wu, wd):
        return (jnp.pad(wg, ((0, 0), (0, fpad))).astype(BF16), jnp.pad(wu, ((0, 0), (0, fpad))).astype(BF16),
                jnp.pad(wd, ((0, fpad), (0, 0))).astype(BF16))

    row = lambda v: v.reshape(1, -1).astype(F32)
    off = 4 * dd
    w_main = jnp.concatenate([w_in[:, :off], w_in[:, off + 4 * DN_HEADS:]], axis=1).astype(BF16)
    beta_w = w_in[:, off:off + 2 * DN_HEADS].reshape(d, 2, DN_HEADS)
    a_w = w_in[:, off + 2 * DN_HEADS:off + 4 * DN_HEADS].reshape(d, 2, DN_HEADS)
    gate_w = jnp.concatenate([beta_w, a_w, jnp.zeros((d, SUBLANES - 4, DN_HEADS), F32)], axis=1)
    wgt = gate_w.transpose(2, 1, 0).reshape(DN_HEADS * SUBLANES, d).astype(BF16)

    def gate_param(v):
        full = jnp.concatenate([jnp.zeros((2, DN_HEADS), F32), v.astype(F32),
                                jnp.zeros((SUBLANES - 4, DN_HEADS), F32)], axis=0)
        return jnp.broadcast_to(full.T.reshape(DN_HEADS * SUBLANES, 1), (DN_HEADS * SUBLANES, tm_inproj))

    gd = (w_in.shape[1] - off - 4 * DN_HEADS) // FN_GROUPS
    idx = jnp.arange(gd, dtype=jnp.int32)
    ang = ((idx[:, None] * idx[None, :]) % gd).astype(F32) * (2.0 * math.pi / gd)
    return dict(
        ffn1=(row(ffn1_norm),) + ffn_w(ffn1_w_gate, ffn1_w_up, ffn1_w_down),
        ffn2=(row(ffn2_norm),) + ffn_w(ffn2_w_gate, ffn2_w_up, ffn2_w_down),
        final=row(final_norm), mix=row(mix_norm), w_main=w_main, wgt=wgt,
        alog=gate_param(a_log), dt=gate_param(dt_bias),
        conv=jnp.pad(conv_w.astype(F32), ((0, SUBLANES - CONV_WIDTH), (0, 0))),
        hn=row(dn_head_norm), chan=(jnp.cos(ang), jnp.sin(ang)),
        wout_a=w_out[:dd].astype(BF16), wout_b=w_out[dd:].astype(BF16),
        xn=row(xattn_norm), mn=row(mem_norm), wq=xattn_w_q.astype(BF16), wkv=xattn_w_kv.astype(BF16),
        wo=xattn_w_o.astype(BF16))


def _gate_pairs(gt, tokens):
    nchunks = tokens // CHUNK
    g = gt.reshape(DN_HEADS, SUBLANES, nchunks, CHUNK)
    pairs = jnp.stack([jnp.concatenate([g[:, 0], g[:, 1]], axis=-1),
                       jnp.concatenate([g[:, 2], g[:, 3]], axis=-1)], axis=1)
    pairs = jnp.pad(pairs, ((0, 0), (0, SUBLANES - 2), (0, 0), (0, 0)))
    return pairs.transpose(2, 0, 1, 3).reshape(nchunks, DN_HEADS * SUBLANES, 2 * CHUNK)


def _trunk(x, mem, w, tm_inproj, final):
    batch, seq, d = x.shape
    t = batch * seq
    dd = DN_HEADS * DN_HEAD_DIM
    x0 = x.reshape(t, d)
    x1 = _ffn(x0, *w["ffn1"], w["final"], final=False)
    p, gt = _inproj(x1, w["mix"], w["w_main"], w["wgt"], w["alog"], w["dt"], tm=tm_inproj)
    gp = _gate_pairs(gt, t)
    o_dn = _deltanet(p.reshape(batch, seq, -1), w["conv"], gp, w["hn"], batch=batch, seq=seq)
    dfn = p.shape[1] - 4 * dd
    scale = (seq * (dfn // FN_GROUPS)) ** -0.5
    wch = (jnp.concatenate(w["chan"], axis=1) * scale).astype(BF16)
    rc, rs = _chandft(p, wch, col_block=4 * dd // dfn)
    ctw, stw = _twiddle(seq)
    o_fn = _seqdft(ctw, stw, rc, rs, batch=batch, seq=seq)
    x2 = _outproj(x1, o_dn.reshape(t, dd), o_fn.reshape(t, dfn), w["wout_a"], w["wout_b"])
    kv = _kvproj(mem, w["mn"], w["wkv"])
    x3 = _xattn(x2, w["xn"], w["wq"], kv, w["wo"], batch=batch, seq=seq)
    y = _ffn(x3.reshape(t, d), *w["ffn2"], w["final"], final=final)
    return y.reshape(batch, seq, d)


def kernel(x_prompt, x_sample, mem_prompt, mem_sample, ffn1_norm, ffn1_w_gate, ffn1_w_up, ffn1_w_down, mix_norm,
           w_in, conv_w, a_log, dt_bias, dn_head_norm, w_out, xattn_norm, mem_norm, xattn_w_q, xattn_w_kv,
           xattn_w_o, ffn2_norm, ffn2_w_gate, ffn2_w_up, ffn2_w_down, final_norm):
    depth = ffn1_norm.shape[0]
    tm_inproj = 512
    y_prompt, y_sample = x_prompt, x_sample
    for l in range(depth):
        w = _prep_weights(ffn1_norm[l], ffn1_w_gate[l], ffn1_w_up[l], ffn1_w_down[l], mix_norm[l], w_in[l],
                          conv_w[l], a_log[l], dt_bias[l], dn_head_norm[l], w_out[l], xattn_norm[l], mem_norm[l],
                          xattn_w_q[l], xattn_w_kv[l], xattn_w_o[l], ffn2_norm[l], ffn2_w_gate[l], ffn2_w_up[l],
                          ffn2_w_down[l], final_norm, tm_inproj=tm_inproj)
        y_prompt = _trunk(y_prompt, mem_prompt, w, tm_inproj, l == depth - 1)
        y_sample = _trunk(y_sample, mem_sample, w, tm_inproj, l == depth - 1)
    return (y_prompt, y_sample)
```

```python
import functools
import math

import jax
import jax.numpy as jnp
from jax import lax
from jax.experimental import pallas as pl
from jax.experimental.pallas import tpu as pltpu

F32 = jnp.float32
BF16 = jnp.bfloat16
EPS = 1e-6

DN_HEADS = 8
DN_HEAD_DIM = 128
FN_GROUPS = 4
CONV_WIDTH = 5
CHUNK = 64
XA_HEADS = 4
N_MEM = 256

LANES = 128
SUBLANES = 8
VMEM_LIMIT = 56 * 1024 * 1024

NT_DIMS = (((1,), (1,)), ((), ()))
TN_DIMS = (((0,), (0,)), ((), ()))


def _params(*sem):
    return pltpu.CompilerParams(dimension_semantics=sem, vmem_limit_bytes=VMEM_LIMIT)


def _dot(a, b):
    return jnp.dot(a, b, preferred_element_type=F32)


def _rms(x, g):
    ms = jnp.mean(x * x, axis=-1, keepdims=True)
    return x * lax.rsqrt(ms + EPS) * g


def _sigmoid(x):
    return 1.0 / (1.0 + jnp.exp(-x))


def _ffn_body(x_ref, g_ref, wg_ref, wu_ref, wd_ref, fin_ref, o_ref, h_ref, acc_ref, *, final):
    j = pl.program_id(1)

    @pl.when(j == 0)
    def _():
        h_ref[...] = _rms(x_ref[...], g_ref[...]).astype(BF16)
        acc_ref[...] = jnp.zeros_like(acc_ref)

    h = h_ref[...]
    a = _dot(h, wg_ref[...])
    u = _dot(h, wu_ref[...])
    act = (a * _sigmoid(a) * u).astype(BF16)
    acc_ref[...] += _dot(act, wd_ref[...])

    @pl.when(j == pl.num_programs(1) - 1)
    def _():
        y = x_ref[...] + 0.5 * acc_ref[...]
        if final:
            y = _rms(y, fin_ref[...])
        o_ref[...] = y


def _ffn(x, g, wg, wu, wd, fin, *, final, tm=512, tf=512):
    t, d = x.shape
    fp = wg.shape[1]
    return pl.pallas_call(
        functools.partial(_ffn_body, final=final),
        grid=(t // tm, fp // tf),
        in_specs=[
            pl.BlockSpec((tm, d), lambda i, j: (i, 0)),
            pl.BlockSpec((1, d), lambda i, j: (0, 0)),
            pl.BlockSpec((d, tf), lambda i, j: (0, j)),
            pl.BlockSpec((d, tf), lambda i, j: (0, j)),
            pl.BlockSpec((tf, d), lambda i, j: (j, 0)),
            pl.BlockSpec((1, d), lambda i, j: (0, 0)),
        ],
        out_specs=pl.BlockSpec((tm, d), lambda i, j: (i, 0)),
        out_shape=jax.ShapeDtypeStruct((t, d), F32),
        scratch_shapes=[pltpu.VMEM((tm, d), BF16), pltpu.VMEM((tm, d), F32)],
        compiler_params=_params("parallel", "arbitrary"),
        name="ffn_final" if final else "ffn",
    )(x, g, wg, wu, wd, fin)


def _inproj_body(x_ref, g_ref, w_ref, wgt_ref, alog_ref, dt_ref, p_ref, gt_ref, h_ref):
    j = pl.program_id(1)

    @pl.when(j == 0)
    def _():
        h = _rms(x_ref[...], g_ref[...]).astype(BF16)
        h_ref[...] = h
        gt = lax.dot_general(wgt_ref[...], h, NT_DIMS, preferred_element_type=F32)
        kind = lax.broadcasted_iota(jnp.int32, gt.shape, 0) & (SUBLANES - 1)
        xs = gt + dt_ref[...]
        softplus = jnp.maximum(xs, 0.0) + jnp.log1p(jnp.exp(-jnp.abs(xs)))
        log_decay = -jnp.exp(alog_ref[...]) * softplus
        gt_ref[...] = jnp.where(kind < 2, _sigmoid(gt), jnp.where(kind < 4, log_decay, 0.0))

    p_ref[...] = _dot(h_ref[...], w_ref[...]).astype(BF16)


def _inproj(x, g, w, wgt, alog, dt, *, tm=512, tn=1024):
    t, d = x.shape
    n = w.shape[1]
    rows = wgt.shape[0]
    return pl.pallas_call(
        _inproj_body,
        grid=(t // tm, n // tn),
        in_specs=[
            pl.BlockSpec((tm, d), lambda i, j: (i, 0)),
            pl.BlockSpec((1, d), lambda i, j: (0, 0)),
            pl.BlockSpec((d, tn), lambda i, j: (0, j)),
            pl.BlockSpec((rows, d), lambda i, j: (0, 0)),
            pl.BlockSpec((rows, tm), lambda i, j: (0, 0)),
            pl.BlockSpec((rows, tm), lambda i, j: (0, 0)),
        ],
        out_specs=[
            pl.BlockSpec((tm, tn), lambda i, j: (i, j)),
            pl.BlockSpec((rows, tm), lambda i, j: (0, i)),
        ],
        out_shape=[jax.ShapeDtypeStruct((t, n), BF16), jax.ShapeDtypeStruct((rows, t), F32)],
        scratch_shapes=[pltpu.VMEM((tm, d), BF16)],
        compiler_params=_params("parallel", "arbitrary"),
        name="inproj",
    )(x, g, w, wgt, alog, dt)


CONV_ROWS = 256
HALO = SUBLANES
PAIR = 2 * CHUNK


def _split2(x):
    hi = x.astype(BF16)
    lo = (x - hi.astype(F32)).astype(BF16)
    return hi, lo


def _aligned(x, m):
    return x if isinstance(x, int) else pl.multiple_of(x, m)


def _deltanet_body(q_ref, k_ref, v_ref, z_ref, cwq_ref, cwk_ref, cwv_ref, gp_ref, hn_ref, o_ref,
                   upad, qs, ks, vs, pq_f, pq_b, n_f, n_b, oc_f, oc_b, el_scr, of, ob, bd1,
                   *, seq, unroll):
    nc = seq // CHUNK
    c = CHUNK
    nblk = nc // unroll

    def conv_stream(src_ref, cw_ref, dst_ref, scale):
        zeros = jnp.zeros((HALO, LANES), F32)
        upad[0:HALO, :] = zeros
        upad[seq + HALO:seq + 2 * HALO, :] = zeros

        def fill(r, _):
            r0 = pl.multiple_of(r * CONV_ROWS, CONV_ROWS)
            upad[pl.ds(r0 + HALO, CONV_ROWS), :] = src_ref[pl.ds(r0, CONV_ROWS), :].astype(F32)
            return 0

        lax.fori_loop(0, seq // CONV_ROWS, fill, 0)
        cw = cw_ref[...]
        n = CONV_ROWS + 2 * HALO

        def block(r, _):
            r0 = pl.multiple_of(r * CONV_ROWS, CONV_ROWS)
            xb = upad[pl.ds(r0, n), :]
            acc = None
            for tap in range(CONV_WIDTH):
                shift = (CONV_WIDTH // 2 - tap) % n
                term = cw[tap:tap + 1, :] * (pltpu.roll(xb, shift, 0) if shift else xb)
                acc = term if acc is None else acc + term
            y = acc[HALO:HALO + CONV_ROWS, :]
            y = y * _sigmoid(y)
            if scale is not None:
                y = y * lax.rsqrt(jnp.sum(y * y, axis=-1, keepdims=True) + EPS) * scale
            dst_ref[pl.ds(r0, CONV_ROWS), :] = y
            return 0

        lax.fori_loop(0, seq // CONV_ROWS, block, 0)

    conv_stream(q_ref, cwq_ref, qs, DN_HEAD_DIM ** -0.5)
    conv_stream(k_ref, cwk_ref, ks, 1.0)
    conv_stream(v_ref, cwv_ref, vs, None)

    ii = lax.broadcasted_iota(jnp.int32, (c, PAIR), 0)
    ll = lax.broadcasted_iota(jnp.int32, (c, PAIR), 1)
    jj = ll & (c - 1)
    is_f = ll < c
    ahead = jnp.where(is_f, ii - jj, jj - ii)
    incl = ahead >= 0
    strict = ahead > 0
    eye = ii == jj
    xor = ii ^ jj

    rr = lax.broadcasted_iota(jnp.int32, (2 * PAIR, 2 * PAIR), 0)
    nn = lax.broadcasted_iota(jnp.int32, (2 * PAIR, 2 * PAIR), 1)
    bd1[...] = jnp.where(((rr & (PAIR - 1)) >> (c.bit_length() - 1)) == (nn >> (PAIR.bit_length() - 1)),
                         1.0, 0.0).astype(BF16)

    def blockdiag(xp):
        top = jnp.where(is_f, xp, 0.0)
        bot = jnp.where(is_f, 0.0, xp)
        return jnp.concatenate([top, bot], axis=0).astype(BF16)

    is_b = ll >= c
    is_f2 = lax.broadcasted_iota(jnp.int32, (2 * c, PAIR), 1) < c
    zer = jnp.zeros((c, LANES), BF16)

    def prep_pair(i, slot):
        ra = _aligned(i * c, c)
        rb = _aligned((nc - 1 - i) * c, c)
        ka = ks[pl.ds(ra, c), :]
        qa = qs[pl.ds(ra, c), :]
        kz = ks[pl.ds(rb, c), :]
        qz = qs[pl.ds(rb, c), :]
        ka16, qa16, kz16, qz16 = ka.astype(BF16), qa.astype(BF16), kz.astype(BF16), qz.astype(BF16)
        va16 = vs[pl.ds(ra, c), :].astype(BF16)
        vz16 = vs[pl.ds(rb, c), :].astype(BF16)
        gram_a = lax.dot_general(jnp.concatenate([ka16, qa16], axis=0), jnp.concatenate([ka16, ka16], axis=0),
                                 NT_DIMS, preferred_element_type=F32)
        gram_z = lax.dot_general(jnp.concatenate([kz16, qz16], axis=0), jnp.concatenate([kz16, kz16], axis=0),
                                 NT_DIMS, preferred_element_type=F32)
        tile = gp_ref[i]
        brow = tile[0:1, :]
        grow = tile[1:2, :]
        x = jnp.concatenate([jnp.where(incl, grow, 0.0), jnp.where(eye, brow, 0.0)], axis=0)
        hi, lo = _split2(x)
        cb = _dot(jnp.concatenate([hi, lo], axis=1), bd1[...])
        yield
        gram = jnp.where(is_f2, gram_a, gram_z)
        cf_b = cb[0:c, 0:PAIR]
        cb_b = cb[0:c, PAIR:2 * PAIR]
        cpair = jnp.where(is_f, cf_b, cb_b)
        bpair = jnp.where(is_f, cb[c:2 * c, 0:PAIR], cb[c:2 * c, PAIR:2 * PAIR])
        crow = jnp.sum(jnp.where(eye, cpair, 0.0), axis=0, keepdims=True)
        decay = jnp.where(incl, jnp.exp(jnp.where(incl, cpair - crow, 0.0)), 0.0)
        a = jnp.where(strict, bpair * gram[0:c, :] * decay, 0.0)
        attn16 = (gram[c:2 * c, :] * decay).astype(BF16)

        t = jnp.where(eye, 1.0, 0.0) - jnp.where(xor == 1, a, 0.0)
        for lvl in range(1, 6):
            e = jnp.where((xor >> lvl) == 1, a, 0.0)
            te = _dot(t.astype(BF16), blockdiag(e))
            yield
            tet = _dot(te.astype(BF16), blockdiag(t))
            yield
            t = t - tet

        erow = jnp.exp(crow)
        bdv = jnp.concatenate([jnp.concatenate([va16, zer], axis=1), jnp.concatenate([zer, vz16], axis=1)], axis=0)
        bdk = jnp.concatenate([jnp.concatenate([ka16, zer], axis=1), jnp.concatenate([zer, kz16], axis=1)], axis=0)
        u = _dot((t * brow).astype(BF16), bdv)
        w = _dot((t * (brow * erow)).astype(BF16), bdk)
        yield
        last_f = cf_b[c - 1:c, :]
        last_b = cb_b[0:1, :]

        p0 = _aligned(slot * (3 * c), c)
        n0 = _aligned(slot * (2 * c), 2 * c)
        o0 = _aligned(slot * c, c)
        dirs = ((0, ka, qa, cf_b, last_f, is_f, pq_f, n_f, oc_f),
                (LANES, kz, qz, cb_b, last_b, is_b, pq_b, n_b, oc_b))
        prods = []
        for (lo, kx, qx, cx, last, keep, pq, nn_ref, oc) in dirs:
            uw = jnp.concatenate([u[:, lo:lo + LANES], w[:, lo:lo + LANES]], axis=1).astype(BF16)
            kdec = (kx * jnp.exp(last - cx)).astype(BF16)
            n_p = lax.dot_general(kdec, uw, TN_DIMS, preferred_element_type=F32)
            o_aw = _dot(jnp.where(keep, attn16, zer), jnp.concatenate([uw, uw], axis=0))
            prods.append((n_p, o_aw))
        yield
        for (lo, kx, qx, cx, last, keep, pq, nn_ref, oc), (n_p, o_aw) in zip(dirs, prods):
            pq[pl.ds(p0, 2 * c), :] = n_p[:, LANES:2 * LANES].astype(BF16)
            pq[pl.ds(p0 + 2 * c, c), :] = (qx * jnp.exp(cx) - o_aw[:, LANES:2 * LANES]).astype(BF16)
            nn_ref[pl.ds(n0, 2 * c), :] = n_p[:, 0:LANES]
            oc[pl.ds(o0, c), :] = o_aw[:, 0:LANES]
        el_scr[slot] = jnp.concatenate([jnp.exp(last_f), jnp.exp(last_b),
                                        jnp.zeros((SUBLANES - 2, LANES), F32)], axis=0)

    def rec_block(blk, state):
        s_f, s_b = state
        for j in range(unroll):
            i = blk * unroll + j
            slot = (blk & 1) * unroll + j
            p0 = _aligned(slot * (3 * c), c)
            n0 = _aligned(slot * (2 * c), 2 * c)
            o0 = _aligned(slot * c, c)
            el = el_scr[slot]
            r_f = _dot(pq_f[pl.ds(p0, 3 * c), :], s_f.astype(BF16))
            r_b = _dot(pq_b[pl.ds(p0, 3 * c), :], s_b.astype(BF16))
            yield
            of[pl.ds(_aligned(i * c, c), c), :] = oc_f[pl.ds(o0, c), :] + r_f[2 * c:3 * c, :]
            s_f = s_f * el[0:1, :] + n_f[pl.ds(n0, 2 * c), :] - r_f[0:2 * c, :]
            ob[pl.ds(_aligned((nc - 1 - i) * c, c), c), :] = oc_b[pl.ds(o0, c), :] + r_b[2 * c:3 * c, :]
            s_b = s_b * el[1:2, :] + n_b[pl.ds(n0, 2 * c), :] - r_b[0:2 * c, :]
        state[0], state[1] = s_f, s_b

    def run_interleaved(gens, rec=None, rec_every=2):
        rounds = 0
        while gens or rec is not None:
            alive = []
            for g in gens:
                try:
                    next(g)
                    alive.append(g)
                except StopIteration:
                    pass
            gens = alive
            if rec is not None and (rounds % rec_every == rec_every - 1 or not gens):
                try:
                    next(rec)
                except StopIteration:
                    rec = None
            rounds += 1

    def prep_gens(blk):
        return [prep_pair(blk * unroll + j, (blk & 1) * unroll + j) for j in range(unroll)]

    def pipelined(blk, carry):
        state = list(carry)
        run_interleaved(prep_gens(blk), rec_block(blk - 1, state))
        return tuple(state)

    run_interleaved(prep_gens(0))
    s0 = jnp.zeros((LANES, LANES), F32)
    state = list(lax.fori_loop(1, nblk, pipelined, (s0, s0)))
    run_interleaved([], rec_block(nblk - 1, state))

    hn = hn_ref[...]

    def finish(r, _):
        r0 = pl.multiple_of(r * CONV_ROWS, CONV_ROWS)
        o = of[pl.ds(r0, CONV_ROWS), :] + ob[pl.ds(r0, CONV_ROWS), :]
        zz = z_ref[pl.ds(r0, CONV_ROWS), :].astype(F32)
        o_ref[pl.ds(r0, CONV_ROWS), :] = (_rms(o, hn) * (zz * _sigmoid(zz))).astype(BF16)
        return 0

    lax.fori_loop(0, seq // CONV_ROWS, finish, 0)


DN_UNROLL = 8


def _deltanet(p, cw, gp, hn, *, batch, seq):
    nc = seq // CHUNK
    unroll = math.gcd(nc, DN_UNROLL)
    h = DN_HEADS
    slots = 2 * unroll
    c = CHUNK
    col = lambda off: pl.BlockSpec((None, seq, LANES), lambda b, hd, off=off: (b, 0, off + hd))
    cws = lambda off: pl.BlockSpec((SUBLANES, LANES), lambda b, hd, off=off: (0, off + hd))
    return pl.pallas_call(
        functools.partial(_deltanet_body, seq=seq, unroll=unroll),
        grid=(batch, h),
        in_specs=[col(0), col(h), col(2 * h), col(3 * h), cws(0), cws(h), cws(2 * h),
                  pl.BlockSpec((nc, SUBLANES, LANES), lambda b, hd: (b, hd, 0)),
                  pl.BlockSpec((1, LANES), lambda b, hd: (0, 0))],
        out_specs=pl.BlockSpec((None, seq, LANES), lambda b, hd: (b, 0, hd)),
        out_shape=jax.ShapeDtypeStruct((batch, seq, h * LANES), BF16),
        scratch_shapes=[
            pltpu.VMEM((seq + 2 * HALO, LANES), F32),
            pltpu.VMEM((seq, LANES), F32),
            pltpu.VMEM((seq, LANES), F32),
            pltpu.VMEM((seq, LANES), F32),
            pltpu.VMEM((slots * 3 * c, LANES), BF16),
            pltpu.VMEM((slots * 3 * c, LANES), BF16),
            pltpu.VMEM((slots * 2 * c, LANES), F32),
            pltpu.VMEM((slots * 2 * c, LANES), F32),
            pltpu.VMEM((slots * c, LANES), F32),
            pltpu.VMEM((slots * c, LANES), F32),
            pltpu.VMEM((slots, SUBLANES, LANES), F32),
            pltpu.VMEM((seq, LANES), F32),
            pltpu.VMEM((seq, LANES), F32),
            pltpu.VMEM((2 * PAIR, 2 * PAIR), BF16),
        ],
        compiler_params=_params("parallel", "parallel"),
        name="deltanet",
    )(p, p, p, p, cw, cw, cw, gp, hn)


TW_ROWS = 64


def _twiddle_body(ac_ref, as_ref, bc_ref, bs_ref, c_ref, s_ref):
    ac = ac_ref[...]
    sn = as_ref[...]
    bc = bc_ref[...]
    bs = bs_ref[...]
    c_ref[...] = (ac * bc - sn * bs).astype(BF16)
    s_ref[...] = (-(sn * bc + ac * bs)).astype(BF16)


def _twiddle(seq):
    nblk = seq // TW_ROWS
    kk = jnp.arange(seq, dtype=jnp.int32)[None, :]
    step = 2.0 * math.pi / seq
    ang_a = ((jnp.arange(TW_ROWS, dtype=jnp.int32)[:, None] * kk) % seq).astype(F32) * step
    ang_b = ((jnp.arange(nblk, dtype=jnp.int32)[:, None] * TW_ROWS * kk) % seq).astype(F32) * step
    tab = pl.BlockSpec((TW_ROWS, seq), lambda i: (0, 0))
    row = pl.BlockSpec((None, 1, seq), lambda i: (i, 0, 0))
    out = pl.BlockSpec((TW_ROWS, seq), lambda i: (i, 0))
    return pl.pallas_call(
        _twiddle_body,
        grid=(nblk,),
        in_specs=[tab, tab, row, row],
        out_specs=[out, out],
        out_shape=[jax.ShapeDtypeStruct((seq, seq), BF16)] * 2,
        compiler_params=_params("parallel"),
        name="twiddle",
    )(jnp.cos(ang_a), jnp.sin(ang_a), jnp.cos(ang_b)[:, None, :], jnp.sin(ang_b)[:, None, :])


def _chandft_body(u_ref, w_ref, rc_ref, rs_ref):
    gd = w_ref.shape[0]
    w = w_ref[...]
    for g in range(FN_GROUPS):
        y = _dot(u_ref[:, g * gd:(g + 1) * gd], w)
        rc_ref[:, g * gd:(g + 1) * gd] = y[:, 0:gd].astype(BF16)
        rs_ref[:, g * gd:(g + 1) * gd] = y[:, gd:2 * gd].astype(BF16)


def _chandft(p, wch, *, col_block, tm=512):
    t = p.shape[0]
    dfn = FN_GROUPS * wch.shape[0]
    out = pl.BlockSpec((tm, dfn), lambda i: (i, 0))
    return pl.pallas_call(
        _chandft_body,
        grid=(t // tm,),
        in_specs=[pl.BlockSpec((tm, dfn), lambda i: (i, col_block)),
                  pl.BlockSpec(wch.shape, lambda i: (0, 0))],
        out_specs=[out, out],
        out_shape=[jax.ShapeDtypeStruct((t, dfn), BF16)] * 2,
        compiler_params=_params("parallel"),
        name="chandft",
    )(p, wch)


def _seqdft_body(c_ref, s_ref, rc_ref, rs_ref, o_ref):
    o_ref[...] = (_dot(c_ref[...], rc_ref[...]) + _dot(s_ref[...], rs_ref[...])).astype(BF16)


def _seqdft(ctw, stw, rc, rs, *, batch, seq, tm=256):
    dfn = rc.shape[-1]
    tw = pl.BlockSpec((tm, seq), lambda b, i: (i, 0))
    rhs = pl.BlockSpec((None, seq, dfn), lambda b, i: (b, 0, 0))
    return pl.pallas_call(
        _seqdft_body,
        grid=(batch, seq // tm),
        in_specs=[tw, tw, rhs, rhs],
        out_specs=pl.BlockSpec((None, tm, dfn), lambda b, i: (b, i, 0)),
        out_shape=jax.ShapeDtypeStruct((batch, seq, dfn), BF16),
        compiler_params=_params("parallel", "arbitrary"),
        name="seqdft",
    )(ctw, stw, rc.reshape(batch, seq, dfn), rs.reshape(batch, seq, dfn))


def _outproj_body(x_ref, a_ref, b_ref, wa_ref, wb_ref, o_ref):
    o_ref[...] = x_ref[...] + _dot(a_ref[...], wa_ref[...]) + _dot(b_ref[...], wb_ref[...])


def _outproj(x, a, b, wa, wb, *, tm=512):
    t, d = x.shape
    da = a.shape[1]
    db = b.shape[1]
    return pl.pallas_call(
        _outproj_body,
        grid=(t // tm,),
        in_specs=[pl.BlockSpec((tm, d), lambda i: (i, 0)),
                  pl.BlockSpec((tm, da), lambda i: (i, 0)),
                  pl.BlockSpec((tm, db), lambda i: (i, 0)),
                  pl.BlockSpec((da, d), lambda i: (0, 0)),
                  pl.BlockSpec((db, d), lambda i: (0, 0))],
        out_specs=pl.BlockSpec((tm, d), lambda i: (i, 0)),
        out_shape=jax.ShapeDtypeStruct((t, d), F32),
        compiler_params=_params("parallel"),
        name="outproj",
    )(x, a, b, wa, wb)


def _kvproj_body(m_ref, g_ref, w_ref, o_ref, h_ref):
    @pl.when(pl.program_id(1) == 0)
    def _():
        h_ref[...] = _rms(m_ref[...], g_ref[...]).astype(BF16)

    o_ref[...] = _dot(h_ref[...], w_ref[...]).astype(BF16)


def _kvproj(mem, g, w, *, tn=1024):
    b, n, d = mem.shape
    nout = w.shape[1]
    return pl.pallas_call(
        _kvproj_body,
        grid=(b, nout // tn),
        in_specs=[pl.BlockSpec((None, n, d), lambda i, j: (i, 0, 0)),
                  pl.BlockSpec((1, d), lambda i, j: (0, 0)),
                  pl.BlockSpec((d, tn), lambda i, j: (0, j))],
        out_specs=pl.BlockSpec((None, n, tn), lambda i, j: (i, 0, j)),
        out_shape=jax.ShapeDtypeStruct((b, n, nout), BF16),
        scratch_shapes=[pltpu.VMEM((n, d), BF16)],
        compiler_params=_params("parallel", "arbitrary"),
        name="kvproj",
    )(mem, g, w)


def _xattn_body(x_ref, g_ref, wq_ref, kv_ref, wo_ref, o_ref, q_scr, a_scr):
    d = x_ref.shape[-1]
    hd = d // XA_HEADS
    x = x_ref[...]
    q_scr[...] = _dot(_rms(x, g_ref[...]).astype(BF16), wq_ref[...]).astype(BF16)
    for h in range(XA_HEADS):
        kh = kv_ref[:, h * hd:(h + 1) * hd]
        vh = kv_ref[:, d + h * hd:d + (h + 1) * hd]
        s = lax.dot_general(q_scr[:, h * hd:(h + 1) * hd], kh, NT_DIMS,
                            preferred_element_type=F32) * (hd ** -0.5)
        p = jnp.exp(s - jnp.max(s, axis=-1, keepdims=True))
        l = jnp.sum(p, axis=-1, keepdims=True)
        a_scr[:, h * hd:(h + 1) * hd] = (_dot(p.astype(BF16), vh) / l).astype(BF16)
    o_ref[...] = x + _dot(a_scr[...], wo_ref[...])


def _xattn(x, g, wq, kv, wo, *, batch, seq, tm=256):
    d = x.shape[-1]
    n = kv.shape[1]
    return pl.pallas_call(
        _xattn_body,
        grid=(batch, seq // tm),
        in_specs=[pl.BlockSpec((None, tm, d), lambda b, i: (b, i, 0)),
                  pl.BlockSpec((1, d), lambda b, i: (0, 0)),
                  pl.BlockSpec((d, d), lambda b, i: (0, 0)),
                  pl.BlockSpec((None, n, 2 * d), lambda b, i: (b, 0, 0)),
                  pl.BlockSpec((d, d), lambda b, i: (0, 0))],
        out_specs=pl.BlockSpec((None, tm, d), lambda b, i: (b, i, 0)),
        out_shape=jax.ShapeDtypeStruct((batch, seq, d), F32),
        scratch_shapes=[pltpu.VMEM((tm, d), BF16), pltpu.VMEM((tm, d), BF16)],
        compiler_params=_params("parallel", "arbitrary"),
        name="xattn",
    )(x.reshape(batch, seq, d), g, wq, kv, wo)


def _prep_weights(ffn1_norm, ffn1_w_gate, ffn1_w_up, ffn1_w_down, mix_norm, w_in, conv_w, a_log, dt_bias,
                  dn_head_norm, w_out, xattn_norm, mem_norm, xattn_w_q, xattn_w_kv, xattn_w_o, ffn2_norm,
                  ffn2_w_gate, ffn2_w_up, ffn2_w_down, final_norm, *, tm_inproj):
    d = w_in.shape[0]
    dd = DN_HEADS * DN_HEAD_DIM
    dff = ffn1_w_gate.shape[1]
    fpad = -dff % 512

    def ffn_w(wg, wu, wd):
        return (jnp.pad(wg, ((0, 0), (0, fpad))).astype(BF16), jnp.pad(wu, ((0, 0), (0, fpad))).astype(BF16),
                jnp.pad(wd, ((0, fpad), (0, 0))).astype(BF16))

    row = lambda v: v.reshape(1, -1).astype(F32)
    off = 4 * dd
    w_main = jnp.concatenate([w_in[:, :off], w_in[:, off + 4 * DN_HEADS:]], axis=1).astype(BF16)
    beta_w = w_in[:, off:off + 2 * DN_HEADS].reshape(d, 2, DN_HEADS)
    a_w = w_in[:, off + 2 * DN_HEADS:off + 4 * DN_HEADS].reshape(d, 2, DN_HEADS)
    gate_w = jnp.concatenate([beta_w, a_w, jnp.zeros((d, SUBLANES - 4, DN_HEADS), F32)], axis=1)
    wgt = gate_w.transpose(2, 1, 0).reshape(DN_HEADS * SUBLANES, d).astype(BF16)

    def gate_param(v):
        full = jnp.concatenate([jnp.zeros((2, DN_HEADS), F32), v.astype(F32),
                                jnp.zeros((SUBLANES - 4, DN_HEADS), F32)], axis=0)
        return jnp.broadcast_to(full.T.reshape(DN_HEADS * SUBLANES, 1), (DN_HEADS * SUBLANES, tm_inproj))

    gd = (w_in.shape[1] - off - 4 * DN_HEADS) // FN_GROUPS
    idx = jnp.arange(gd, dtype=jnp.int32)
    ang = ((idx[:, None] * idx[None, :]) % gd).astype(F32) * (2.0 * math.pi / gd)
    return dict(
        ffn1=(row(ffn1_norm),) + ffn_w(ffn1_w_gate, ffn1_w_up, ffn1_w_down),
        ffn2=(row(ffn2_norm),) + ffn_w(ffn2_w_gate, ffn2_w_up, ffn2_w_down),
        final=row(final_norm), mix=row(mix_norm), w_main=w_main, wgt=wgt,
        alog=gate_param(a_log), dt=gate_param(dt_bias),
        conv=jnp.pad(conv_w.astype(F32), ((0, SUBLANES - CONV_WIDTH), (0, 0))),
        hn=row(dn_head_norm), chan=(jnp.cos(ang), jnp.sin(ang)),
        wout_a=w_out[:dd].astype(BF16), wout_b=w_out[dd:].astype(BF16),
        xn=row(xattn_norm), mn=row(mem_norm), wq=xattn_w_q.astype(BF16), wkv=xattn_w_kv.astype(BF16),
        wo=xattn_w_o.astype(BF16))


def _gate_pairs(gt, batch, seq):
    nc = seq // CHUNK
    g = gt.reshape(DN_HEADS, SUBLANES, batch, nc, CHUNK)
    rev = lambda a: a[:, :, ::-1, :]
    pairs = jnp.stack([jnp.concatenate([g[:, 0], rev(g[:, 1])], axis=-1),
                       jnp.concatenate([g[:, 2], rev(g[:, 3])], axis=-1)], axis=1)
    pairs = jnp.pad(pairs, ((0, 0), (0, SUBLANES - 2), (0, 0), (0, 0), (0, 0)))
    return pairs.transpose(2, 3, 0, 1, 4).reshape(batch * nc, DN_HEADS * SUBLANES, 2 * CHUNK)


def _trunk(x, mem, w, tm_inproj, final):
    batch, seq, d = x.shape
    t = batch * seq
    dd = DN_HEADS * DN_HEAD_DIM
    x0 = x.reshape(t, d)
    x1 = _ffn(x0, *w["ffn1"], w["final"], final=False)
    p, gt = _inproj(x1, w["mix"], w["w_main"], w["wgt"], w["alog"], w["dt"], tm=tm_inproj)
    gp = _gate_pairs(gt, batch, seq)
    o_dn = _deltanet(p.reshape(batch, seq, -1), w["conv"], gp, w["hn"], batch=batch, seq=seq)
    dfn = p.shape[1] - 4 * dd
    scale = (seq * (dfn // FN_GROUPS)) ** -0.5
    wch = (jnp.concatenate(w["chan"], axis=1) * scale).astype(BF16)
    rc, rs = _chandft(p, wch, col_block=4 * dd // dfn)
    ctw, stw = _twiddle(seq)
    o_fn = _seqdft(ctw, stw, rc, rs, batch=batch, seq=seq)
    x2 = _outproj(x1, o_dn.reshape(t, dd), o_fn.reshape(t, dfn), w["wout_a"], w["wout_b"])
    kv = _kvproj(mem, w["mn"], w["wkv"])
    x3 = _xattn(x2, w["xn"], w["wq"], kv, w["wo"], batch=batch, seq=seq)
    y = _ffn(x3.reshape(t, d), *w["ffn2"], w["final"], final=final)
    return y.reshape(batch, seq, d)


def kernel(x_prompt, x_sample, mem_prompt, mem_sample, ffn1_norm, ffn1_w_gate, ffn1_w_up, ffn1_w_down, mix_norm,
           w_in, conv_w, a_log, dt_bias, dn_head_norm, w_out, xattn_norm, mem_norm, xattn_w_q, xattn_w_kv,
           xattn_w_o, ffn2_norm, ffn2_w_gate, ffn2_w_up, ffn2_w_down, final_norm):
    depth = ffn1_norm.shape[0]
    tm_inproj = 512
    y_prompt, y_sample = x_prompt, x_sample
    for l in range(depth):
        w = _prep_weights(ffn1_norm[l], ffn1_w_gate[l], ffn1_w_up[l], ffn1_w_down[l], mix_norm[l], w_in[l],
                          conv_w[l], a_log[l], dt_bias[l], dn_head_norm[l], w_out[l], xattn_norm[l], mem_norm[l],
                          xattn_w_q[l], xattn_w_kv[l], xattn_w_o[l], ffn2_norm[l], ffn2_w_gate[l], ffn2_w_up[l],
                          ffn2_w_down[l], final_norm, tm_inproj=tm_inproj)
        y_prompt = _trunk(y_prompt, mem_prompt, w, tm_inproj, l == depth - 1)
        y_sample = _trunk(y_sample, mem_sample, w, tm_inproj, l == depth - 1)
    return (y_prompt, y_sample)
```

```python
import functools
import math

import jax
import jax.numpy as jnp
from jax import lax
from jax.experimental import pallas as pl
from jax.experimental.pallas import tpu as pltpu

F32 = jnp.float32
BF16 = jnp.bfloat16
EPS = 1e-6

DN_HEADS = 8
DN_HEAD_DIM = 128
FN_GROUPS = 4
CONV_WIDTH = 5
CHUNK = 64
XA_HEADS = 4
N_MEM = 256

LANES = 128
SUBLANES = 8
VMEM_LIMIT = 56 * 1024 * 1024

NT_DIMS = (((1,), (1,)), ((), ()))
TN_DIMS = (((0,), (0,)), ((), ()))


def _params(*sem):
    return pltpu.CompilerParams(dimension_semantics=sem, vmem_limit_bytes=VMEM_LIMIT)


def _dot(a, b):
    return jnp.dot(a, b, preferred_element_type=F32)


def _rms(x, g):
    ms = jnp.mean(x * x, axis=-1, keepdims=True)
    return x * lax.rsqrt(ms + EPS) * g


def _sigmoid(x):
    return 1.0 / (1.0 + jnp.exp(-x))


def _ffn_body(x_ref, g_ref, wg_ref, wu_ref, wd_ref, fin_ref, o_ref, h_ref, acc_ref, *, final, last_width):
    j = pl.program_id(1)
    nj = pl.num_programs(1)
    tf = wg_ref.shape[1]

    @pl.when(j == 0)
    def _():
        h_ref[...] = _rms(x_ref[...], g_ref[...]).astype(BF16)
        acc_ref[...] = jnp.zeros_like(acc_ref)

    def hidden_block(width):
        h = h_ref[...]
        a = _dot(h, wg_ref[:, 0:width])
        u = _dot(h, wu_ref[:, 0:width])
        act = (a * _sigmoid(a) * u).astype(BF16)
        acc_ref[...] += _dot(act, wd_ref[0:width, :])

    if last_width == tf:
        hidden_block(tf)
    else:
        pl.when(j < nj - 1)(lambda: hidden_block(tf))
        pl.when(j == nj - 1)(lambda: hidden_block(last_width))

    @pl.when(j == nj - 1)
    def _():
        y = x_ref[...] + 0.5 * acc_ref[...]
        if final:
            y = _rms(y, fin_ref[...])
        o_ref[...] = y


def _ffn(x, g, wg, wu, wd, fin, *, final, tm=512, tf=512):
    t, d = x.shape
    dff = wg.shape[1]
    nj = pl.cdiv(dff, tf)
    last_width = dff - (nj - 1) * tf
    assert last_width % LANES == 0, (dff, tf)
    return pl.pallas_call(
        functools.partial(_ffn_body, final=final, last_width=last_width),
        grid=(t // tm, nj),
        in_specs=[
            pl.BlockSpec((tm, d), lambda i, j: (i, 0)),
            pl.BlockSpec((1, d), lambda i, j: (0, 0)),
            pl.BlockSpec((d, tf), lambda i, j: (0, j)),
            pl.BlockSpec((d, tf), lambda i, j: (0, j)),
            pl.BlockSpec((tf, d), lambda i, j: (j, 0)),
            pl.BlockSpec((1, d), lambda i, j: (0, 0)),
        ],
        out_specs=pl.BlockSpec((tm, d), lambda i, j: (i, 0)),
        out_shape=jax.ShapeDtypeStruct((t, d), F32),
        scratch_shapes=[pltpu.VMEM((tm, d), BF16), pltpu.VMEM((tm, d), F32)],
        compiler_params=_params("parallel", "arbitrary"),
        name="ffn_final" if final else "ffn",
    )(x, g, wg, wu, wd, fin)


def _inproj_body(x_ref, g_ref, w_ref, wgt_ref, alog_ref, dt_ref, p_ref, gt_ref, h_ref):
    j = pl.program_id(1)

    @pl.when(j == 0)
    def _():
        h = _rms(x_ref[...], g_ref[...]).astype(BF16)
        h_ref[...] = h
        gt = lax.dot_general(wgt_ref[...], h, NT_DIMS, preferred_element_type=F32)
        kind = lax.broadcasted_iota(jnp.int32, gt.shape, 0) & (SUBLANES - 1)
        xs = gt + dt_ref[...]
        softplus = jnp.maximum(xs, 0.0) + jnp.log1p(jnp.exp(-jnp.abs(xs)))
        log_decay = -jnp.exp(alog_ref[...]) * softplus
        gt_ref[...] = jnp.where(kind < 2, _sigmoid(gt), jnp.where(kind < 4, log_decay, 0.0))

    p_ref[...] = _dot(h_ref[...], w_ref[...]).astype(BF16)


def _inproj(x, g, w, wgt, alog, dt, *, tm=512, tn=2560):
    t, d = x.shape
    n = w.shape[1]
    rows = wgt.shape[0]
    return pl.pallas_call(
        _inproj_body,
        grid=(t // tm, n // tn),
        in_specs=[
            pl.BlockSpec((tm, d), lambda i, j: (i, 0)),
            pl.BlockSpec((1, d), lambda i, j: (0, 0)),
            pl.BlockSpec((d, tn), lambda i, j: (0, j)),
            pl.BlockSpec((rows, d), lambda i, j: (0, 0)),
            pl.BlockSpec((rows, tm), lambda i, j: (0, 0)),
            pl.BlockSpec((rows, tm), lambda i, j: (0, 0)),
        ],
        out_specs=[
            pl.BlockSpec((tm, tn), lambda i, j: (i, j)),
            pl.BlockSpec((rows, tm), lambda i, j: (0, i)),
        ],
        out_shape=[jax.ShapeDtypeStruct((t, n), BF16), jax.ShapeDtypeStruct((rows, t), F32)],
        scratch_shapes=[pltpu.VMEM((tm, d), BF16)],
        compiler_params=_params("parallel", "arbitrary"),
        name="inproj",
    )(x, g, w, wgt, alog, dt)


CONV_ROWS = 512
HALO = SUBLANES
PAIR = 2 * CHUNK


def _split2(x):
    hi = x.astype(BF16)
    lo = (x - hi.astype(F32)).astype(BF16)
    return hi, lo


def _aligned(x, m):
    return x if isinstance(x, int) else pl.multiple_of(x, m)


def _advance(gen):
    try:
        next(gen)
        return True
    except StopIteration:
        return False


PREP_STAGES = 14


def _deltanet_body(q_ref, k_ref, v_ref, z_ref, cwq_ref, cwk_ref, cwv_ref, gp_ref, hn_ref, o_ref,
                   upad, qs, ks, vs, pq_f, pq_b, n_f, n_b, oc_f, oc_b, el_scr, of, ob, bd1,
                   *, seq, unroll):
    nc = seq // CHUNK
    c = CHUNK
    nblk = nc // unroll

    def conv_stream(src_ref, cw_ref, dst_ref, scale):
        zeros = jnp.zeros((HALO, LANES), F32)
        upad[0:HALO, :] = zeros
        upad[seq + HALO:seq + 2 * HALO, :] = zeros

        def fill(r, _):
            r0 = pl.multiple_of(r * CONV_ROWS, CONV_ROWS)
            upad[pl.ds(r0 + HALO, CONV_ROWS), :] = src_ref[pl.ds(r0, CONV_ROWS), :].astype(F32)
            return 0

        lax.fori_loop(0, seq // CONV_ROWS, fill, 0)
        cw = cw_ref[...]

        def block(r, _):
            r0 = pl.multiple_of(r * CONV_ROWS, CONV_ROWS)
            y = None
            for tap in range(CONV_WIDTH):
                term = cw[tap:tap + 1, :] * upad[pl.ds(r0 + (HALO + tap - CONV_WIDTH // 2), CONV_ROWS), :]
                y = term if y is None else y + term
            y = y * _sigmoid(y)
            if scale is not None:
                y = y * lax.rsqrt(jnp.sum(y * y, axis=-1, keepdims=True) + EPS)
                if scale != 1.0:
                    y = y * scale
            dst_ref[pl.ds(r0, CONV_ROWS), :] = y
            return 0

        lax.fori_loop(0, seq // CONV_ROWS, block, 0)

    conv_stream(q_ref, cwq_ref, qs, DN_HEAD_DIM ** -0.5)
    conv_stream(k_ref, cwk_ref, ks, 1.0)
    conv_stream(v_ref, cwv_ref, vs, None)

    ii = lax.broadcasted_iota(jnp.int32, (c, PAIR), 0)
    ll = lax.broadcasted_iota(jnp.int32, (c, PAIR), 1)
    jj = ll & (c - 1)
    is_f = ll < c
    ahead = jnp.where(is_f, ii - jj, jj - ii)
    incl = ahead >= 0
    strict = ahead > 0
    eye = ii == jj
    xor = ii ^ jj

    rr = lax.broadcasted_iota(jnp.int32, (2 * PAIR, 2 * PAIR), 0)
    nn = lax.broadcasted_iota(jnp.int32, (2 * PAIR, 2 * PAIR), 1)
    bd1[...] = jnp.where(((rr & (PAIR - 1)) >> (c.bit_length() - 1)) == (nn >> (PAIR.bit_length() - 1)),
                         1.0, 0.0).astype(BF16)

    def blockdiag(xp):
        top = jnp.where(is_f, xp, 0.0)
        bot = jnp.where(is_f, 0.0, xp)
        return jnp.concatenate([top, bot], axis=0).astype(BF16)

    is_b = ll >= c
    is_f2 = lax.broadcasted_iota(jnp.int32, (2 * c, PAIR), 1) < c
    zer = jnp.zeros((c, LANES), BF16)

    def prep_pair(i, slot):
        ra = _aligned(i * c, c)
        rb = _aligned((nc - 1 - i) * c, c)
        ka = ks[pl.ds(ra, c), :]
        qa = qs[pl.ds(ra, c), :]
        kz = ks[pl.ds(rb, c), :]
        qz = qs[pl.ds(rb, c), :]
        ka16, qa16, kz16, qz16 = ka.astype(BF16), qa.astype(BF16), kz.astype(BF16), qz.astype(BF16)
        va16 = vs[pl.ds(ra, c), :].astype(BF16)
        vz16 = vs[pl.ds(rb, c), :].astype(BF16)
        gram_a = lax.dot_general(jnp.concatenate([ka16, qa16], axis=0), jnp.concatenate([ka16, ka16], axis=0),
                                 NT_DIMS, preferred_element_type=F32)
        gram_z = lax.dot_general(jnp.concatenate([kz16, qz16], axis=0), jnp.concatenate([kz16, kz16], axis=0),
                                 NT_DIMS, preferred_element_type=F32)
        tile = gp_ref[i]
        brow = tile[0:1, :]
        grow = tile[1:2, :]
        x = jnp.concatenate([jnp.where(incl, grow, 0.0), jnp.where(eye, brow, 0.0)], axis=0)
        hi, lo = _split2(x)
        cb = _dot(jnp.concatenate([hi, lo], axis=1), bd1[...])
        yield
        gram = jnp.where(is_f2, gram_a, gram_z)
        cf_b = cb[0:c, 0:PAIR]
        cb_b = cb[0:c, PAIR:2 * PAIR]
        cpair = jnp.where(is_f, cf_b, cb_b)
        bpair = jnp.where(is_f, cb[c:2 * c, 0:PAIR], cb[c:2 * c, PAIR:2 * PAIR])
        crow = jnp.sum(jnp.where(eye, cpair, 0.0), axis=0, keepdims=True)
        decay = jnp.where(incl, jnp.exp(jnp.where(incl, cpair - crow, 0.0)), 0.0)
        a = jnp.where(strict, bpair * gram[0:c, :] * decay, 0.0)
        attn16 = (gram[c:2 * c, :] * decay).astype(BF16)

        t = jnp.where(eye, 1.0, 0.0) - jnp.where(xor == 1, a, 0.0)
        for lvl in range(1, 6):
            e = jnp.where((xor >> lvl) == 1, a, 0.0)
            te = _dot(t.astype(BF16), blockdiag(e))
            yield
            tet = _dot(te.astype(BF16), blockdiag(t))
            yield
            t = t - tet

        erow = jnp.exp(crow)
        bdv = jnp.concatenate([jnp.concatenate([va16, zer], axis=1), jnp.concatenate([zer, vz16], axis=1)], axis=0)
        bdk = jnp.concatenate([jnp.concatenate([ka16, zer], axis=1), jnp.concatenate([zer, kz16], axis=1)], axis=0)
        u = _dot((t * brow).astype(BF16), bdv)
        w = _dot((t * (brow * erow)).astype(BF16), bdk)
        yield
        last_f = cf_b[c - 1:c, :]
        last_b = cb_b[0:1, :]

        p0 = _aligned(slot * (3 * c), c)
        n0 = _aligned(slot * (2 * c), 2 * c)
        o0 = _aligned(slot * c, c)
        dirs = ((0, ka, qa, cf_b, last_f, is_f, pq_f, n_f, oc_f),
                (LANES, kz, qz, cb_b, last_b, is_b, pq_b, n_b, oc_b))
        prods = []
        for (lo, kx, qx, cx, last, keep, pq, nn_ref, oc) in dirs:
            uw = jnp.concatenate([u[:, lo:lo + LANES], w[:, lo:lo + LANES]], axis=1).astype(BF16)
            kdec = (kx * jnp.exp(last - cx)).astype(BF16)
            n_p = lax.dot_general(kdec, uw, TN_DIMS, preferred_element_type=F32)
            o_aw = _dot(jnp.where(keep, attn16, zer), jnp.concatenate([uw, uw], axis=0))
            prods.append((n_p, o_aw))
        yield
        for (lo, kx, qx, cx, last, keep, pq, nn_ref, oc), (n_p, o_aw) in zip(dirs, prods):
            pq[pl.ds(p0, 2 * c), :] = n_p[:, LANES:2 * LANES].astype(BF16)
            pq[pl.ds(p0 + 2 * c, c), :] = (qx * jnp.exp(cx) - o_aw[:, LANES:2 * LANES]).astype(BF16)
            nn_ref[pl.ds(n0, 2 * c), :] = n_p[:, 0:LANES]
            oc[pl.ds(o0, c), :] = o_aw[:, 0:LANES]
        el_scr[slot] = jnp.concatenate([jnp.exp(last_f), jnp.exp(last_b),
                                        jnp.zeros((SUBLANES - 2, LANES), F32)], axis=0)

    def rec_block(blk, state):
        s_f, s_b = state
        for j in range(unroll):
            i = blk * unroll + j
            slot = (blk & 1) * unroll + j
            p0 = _aligned(slot * (3 * c), c)
            n0 = _aligned(slot * (2 * c), 2 * c)
            o0 = _aligned(slot * c, c)
            el = el_scr[slot]
            r_f = _dot(pq_f[pl.ds(p0, 3 * c), :], s_f.astype(BF16))
            r_b = _dot(pq_b[pl.ds(p0, 3 * c), :], s_b.astype(BF16))
            yield
            of[pl.ds(_aligned(i * c, c), c), :] = oc_f[pl.ds(o0, c), :] + r_f[2 * c:3 * c, :]
            s_f = s_f * el[0:1, :] + n_f[pl.ds(n0, 2 * c), :] - r_f[0:2 * c, :]
            ob[pl.ds(_aligned((nc - 1 - i) * c, c), c), :] = oc_b[pl.ds(o0, c), :] + r_b[2 * c:3 * c, :]
            s_b = s_b * el[1:2, :] + n_b[pl.ds(n0, 2 * c), :] - r_b[0:2 * c, :]
        state[0], state[1] = s_f, s_b

    def run_interleaved(gens, rec=None):
        rec_period = max(1, (PREP_STAGES * unroll) // (unroll + 1))
        steps = 0
        while gens or rec is not None:
            alive = []
            for g in gens:
                if _advance(g):
                    alive.append(g)
                steps += 1
                if rec is not None and steps % rec_period == 0 and not _advance(rec):
                    rec = None
            gens = alive
            if not gens and rec is not None and not _advance(rec):
                rec = None

    def prep_gens(blk):
        return [prep_pair(blk * unroll + j, (blk & 1) * unroll + j) for j in range(unroll)]

    def pipelined(blk, carry):
        state = list(carry)
        run_interleaved(prep_gens(blk), rec_block(blk - 1, state))
        return tuple(state)

    run_interleaved(prep_gens(0))
    s0 = jnp.zeros((LANES, LANES), F32)
    state = list(lax.fori_loop(1, nblk, pipelined, (s0, s0)))
    run_interleaved([], rec_block(nblk - 1, state))

    hn = hn_ref[...]

    def finish(r, _):
        r0 = pl.multiple_of(r * CONV_ROWS, CONV_ROWS)
        o = of[pl.ds(r0, CONV_ROWS), :] + ob[pl.ds(r0, CONV_ROWS), :]
        zz = z_ref[pl.ds(r0, CONV_ROWS), :].astype(F32)
        o_ref[pl.ds(r0, CONV_ROWS), :] = (_rms(o, hn) * (zz * _sigmoid(zz))).astype(BF16)
        return 0

    lax.fori_loop(0, seq // CONV_ROWS, finish, 0)


DN_UNROLL = 16
DN_MIN_BLOCKS = 4


def _deltanet(p, cw, gp, hn, *, batch, seq):
    assert seq % CONV_ROWS == 0 and CONV_ROWS % CHUNK == 0, (seq, CONV_ROWS)
    nc = seq // CHUNK
    unroll = math.gcd(nc, DN_UNROLL)
    while unroll > 1 and nc // unroll < DN_MIN_BLOCKS:
        unroll //= 2
    h = DN_HEADS
    slots = 2 * unroll
    c = CHUNK
    col = lambda off: pl.BlockSpec((None, seq, LANES), lambda b, hd, off=off: (b, 0, off + hd))
    cws = lambda off: pl.BlockSpec((SUBLANES, LANES), lambda b, hd, off=off: (0, off + hd))
    return pl.pallas_call(
        functools.partial(_deltanet_body, seq=seq, unroll=unroll),
        grid=(batch, h),
        in_specs=[col(0), col(h), col(2 * h), col(3 * h), cws(0), cws(h), cws(2 * h),
                  pl.BlockSpec((nc, SUBLANES, LANES), lambda b, hd: (b, hd, 0)),
                  pl.BlockSpec((1, LANES), lambda b, hd: (0, 0))],
        out_specs=pl.BlockSpec((None, seq, LANES), lambda b, hd: (b, 0, hd)),
        out_shape=jax.ShapeDtypeStruct((batch, seq, h * LANES), BF16),
        scratch_shapes=[
            pltpu.VMEM((seq + 2 * HALO, LANES), F32),
            pltpu.VMEM((seq, LANES), F32),
            pltpu.VMEM((seq, LANES), F32),
            pltpu.VMEM((seq, LANES), F32),
            pltpu.VMEM((slots * 3 * c, LANES), BF16),
            pltpu.VMEM((slots * 3 * c, LANES), BF16),
            pltpu.VMEM((slots * 2 * c, LANES), F32),
            pltpu.VMEM((slots * 2 * c, LANES), F32),
            pltpu.VMEM((slots * c, LANES), F32),
            pltpu.VMEM((slots * c, LANES), F32),
            pltpu.VMEM((slots, SUBLANES, LANES), F32),
            pltpu.VMEM((seq, LANES), F32),
            pltpu.VMEM((seq, LANES), F32),
            pltpu.VMEM((2 * PAIR, 2 * PAIR), BF16),
        ],
        compiler_params=_params("parallel", "parallel"),
        name="deltanet",
    )(p, p, p, p, cw, cw, cw, gp, hn)


TW_ROWS = 64


def _twiddle_body(ac_ref, as_ref, bc_ref, bs_ref, c_ref, s_ref):
    ac = ac_ref[...]
    sn = as_ref[...]
    bc = bc_ref[...]
    bs = bs_ref[...]
    c_ref[...] = (ac * bc - sn * bs).astype(BF16)
    s_ref[...] = (-(sn * bc + ac * bs)).astype(BF16)


def _twiddle(seq):
    nblk = seq // TW_ROWS
    kk = jnp.arange(seq, dtype=jnp.int32)[None, :]
    step = 2.0 * math.pi / seq
    ang_a = ((jnp.arange(TW_ROWS, dtype=jnp.int32)[:, None] * kk) % seq).astype(F32) * step
    ang_b = ((jnp.arange(nblk, dtype=jnp.int32)[:, None] * TW_ROWS * kk) % seq).astype(F32) * step
    tab = pl.BlockSpec((TW_ROWS, seq), lambda i: (0, 0))
    row = pl.BlockSpec((None, 1, seq), lambda i: (i, 0, 0))
    out = pl.BlockSpec((TW_ROWS, seq), lambda i: (i, 0))
    return pl.pallas_call(
        _twiddle_body,
        grid=(nblk,),
        in_specs=[tab, tab, row, row],
        out_specs=[out, out],
        out_shape=[jax.ShapeDtypeStruct((seq, seq), BF16)] * 2,
        compiler_params=_params("parallel"),
        name="twiddle",
    )(jnp.cos(ang_a), jnp.sin(ang_a), jnp.cos(ang_b)[:, None, :], jnp.sin(ang_b)[:, None, :])


def _chandft_body(u_ref, w_ref, rc_ref, rs_ref):
    gd = w_ref.shape[0]
    w = w_ref[...]
    for g in range(FN_GROUPS):
        y = _dot(u_ref[:, g * gd:(g + 1) * gd], w)
        rc_ref[:, g * gd:(g + 1) * gd] = y[:, 0:gd].astype(BF16)
        rs_ref[:, g * gd:(g + 1) * gd] = y[:, gd:2 * gd].astype(BF16)


def _chandft(p, wch, *, col_block, tm=512):
    t = p.shape[0]
    dfn = FN_GROUPS * wch.shape[0]
    out = pl.BlockSpec((tm, dfn), lambda i: (i, 0))
    return pl.pallas_call(
        _chandft_body,
        grid=(t // tm,),
        in_specs=[pl.BlockSpec((tm, dfn), lambda i: (i, col_block)),
                  pl.BlockSpec(wch.shape, lambda i: (0, 0))],
        out_specs=[out, out],
        out_shape=[jax.ShapeDtypeStruct((t, dfn), BF16)] * 2,
        compiler_params=_params("parallel"),
        name="chandft",
    )(p, wch)


def _seqdft_body(c_ref, s_ref, rc_ref, rs_ref, o_ref):
    o_ref[...] = (_dot(c_ref[...], rc_ref[...]) + _dot(s_ref[...], rs_ref[...])).astype(BF16)


def _seqdft(ctw, stw, rc, rs, *, batch, seq, tm=256):
    dfn = rc.shape[-1]
    tw = pl.BlockSpec((tm, seq), lambda b, i: (i, 0))
    rhs = pl.BlockSpec((None, seq, dfn), lambda b, i: (b, 0, 0))
    return pl.pallas_call(
        _seqdft_body,
        grid=(batch, seq // tm),
        in_specs=[tw, tw, rhs, rhs],
        out_specs=pl.BlockSpec((None, tm, dfn), lambda b, i: (b, i, 0)),
        out_shape=jax.ShapeDtypeStruct((batch, seq, dfn), BF16),
        compiler_params=_params("parallel", "arbitrary"),
        name="seqdft",
    )(ctw, stw, rc.reshape(batch, seq, dfn), rs.reshape(batch, seq, dfn))


def _outproj_body(x_ref, a_ref, b_ref, wa_ref, wb_ref, o_ref):
    o_ref[...] = x_ref[...] + _dot(a_ref[...], wa_ref[...]) + _dot(b_ref[...], wb_ref[...])


def _outproj(x, a, b, wa, wb, *, tm=512):
    t, d = x.shape
    da = a.shape[1]
    db = b.shape[1]
    return pl.pallas_call(
        _outproj_body,
        grid=(t // tm,),
        in_specs=[pl.BlockSpec((tm, d), lambda i: (i, 0)),
                  pl.BlockSpec((tm, da), lambda i: (i, 0)),
                  pl.BlockSpec((tm, db), lambda i: (i, 0)),
                  pl.BlockSpec((da, d), lambda i: (0, 0)),
                  pl.BlockSpec((db, d), lambda i: (0, 0))],
        out_specs=pl.BlockSpec((tm, d), lambda i: (i, 0)),
        out_shape=jax.ShapeDtypeStruct((t, d), F32),
        compiler_params=_params("parallel"),
        name="outproj",
    )(x, a, b, wa, wb)


def _kvproj_body(m_ref, g_ref, w_ref, o_ref, h_ref):
    @pl.when(pl.program_id(1) == 0)
    def _():
        h_ref[...] = _rms(m_ref[...], g_ref[...]).astype(BF16)

    o_ref[...] = _dot(h_ref[...], w_ref[...]).astype(BF16)


def _kvproj(mem, g, w, *, tn=1024):
    b, n, d = mem.shape
    nout = w.shape[1]
    return pl.pallas_call(
        _kvproj_body,
        grid=(b, nout // tn),
        in_specs=[pl.BlockSpec((None, n, d), lambda i, j: (i, 0, 0)),
                  pl.BlockSpec((1, d), lambda i, j: (0, 0)),
                  pl.BlockSpec((d, tn), lambda i, j: (0, j))],
        out_specs=pl.BlockSpec((None, n, tn), lambda i, j: (i, 0, j)),
        out_shape=jax.ShapeDtypeStruct((b, n, nout), BF16),
        scratch_shapes=[pltpu.VMEM((n, d), BF16)],
        compiler_params=_params("parallel", "arbitrary"),
        name="kvproj",
    )(mem, g, w)


def _xattn_body(x_ref, g_ref, wq_ref, kv_ref, wo_ref, o_ref, q_scr, a_scr):
    d = x_ref.shape[-1]
    hd = d // XA_HEADS
    x = x_ref[...]
    q_scr[...] = _dot(_rms(x, g_ref[...]).astype(BF16), wq_ref[...]).astype(BF16)
    for h in range(XA_HEADS):
        kh = kv_ref[:, h * hd:(h + 1) * hd]
        vh = kv_ref[:, d + h * hd:d + (h + 1) * hd]
        s = lax.dot_general(q_scr[:, h * hd:(h + 1) * hd], kh, NT_DIMS,
                            preferred_element_type=F32) * (hd ** -0.5)
        p = jnp.exp(s - jnp.max(s, axis=-1, keepdims=True))
        l = jnp.sum(p, axis=-1, keepdims=True)
        a_scr[:, h * hd:(h + 1) * hd] = (_dot(p.astype(BF16), vh) / l).astype(BF16)
    o_ref[...] = x + _dot(a_scr[...], wo_ref[...])


def _xattn(x, g, wq, kv, wo, *, batch, seq, tm=256):
    d = x.shape[-1]
    n = kv.shape[1]
    return pl.pallas_call(
        _xattn_body,
        grid=(batch, seq // tm),
        in_specs=[pl.BlockSpec((None, tm, d), lambda b, i: (b, i, 0)),
                  pl.BlockSpec((1, d), lambda b, i: (0, 0)),
                  pl.BlockSpec((d, d), lambda b, i: (0, 0)),
                  pl.BlockSpec((None, n, 2 * d), lambda b, i: (b, 0, 0)),
                  pl.BlockSpec((d, d), lambda b, i: (0, 0))],
        out_specs=pl.BlockSpec((None, tm, d), lambda b, i: (b, i, 0)),
        out_shape=jax.ShapeDtypeStruct((batch, seq, d), F32),
        scratch_shapes=[pltpu.VMEM((tm, d), BF16), pltpu.VMEM((tm, d), BF16)],
        compiler_params=_params("parallel", "arbitrary"),
        name="xattn",
    )(x.reshape(batch, seq, d), g, wq, kv, wo)


def _prep_weights(ffn1_norm, ffn1_w_gate, ffn1_w_up, ffn1_w_down, mix_norm, w_in, conv_w, a_log, dt_bias,
                  dn_head_norm, w_out, xattn_norm, mem_norm, xattn_w_q, xattn_w_kv, xattn_w_o, ffn2_norm,
                  ffn2_w_gate, ffn2_w_up, ffn2_w_down, final_norm, *, tm_inproj):
    d = w_in.shape[0]
    dd = DN_HEADS * DN_HEAD_DIM

    def ffn_w(wg, wu, wd):
        return (wg.astype(BF16), wu.astype(BF16), wd.astype(BF16))

    row = lambda v: v.reshape(1, -1).astype(F32)
    off = 4 * dd
    w_main = jnp.concatenate([w_in[:, :off], w_in[:, off + 4 * DN_HEADS:]], axis=1).astype(BF16)
    beta_w = w_in[:, off:off + 2 * DN_HEADS].reshape(d, 2, DN_HEADS)
    a_w = w_in[:, off + 2 * DN_HEADS:off + 4 * DN_HEADS].reshape(d, 2, DN_HEADS)
    gate_w = jnp.concatenate([beta_w, a_w, jnp.zeros((d, SUBLANES - 4, DN_HEADS), F32)], axis=1)
    wgt = gate_w.transpose(2, 1, 0).reshape(DN_HEADS * SUBLANES, d).astype(BF16)

    def gate_param(v):
        full = jnp.concatenate([jnp.zeros((2, DN_HEADS), F32), v.astype(F32),
                                jnp.zeros((SUBLANES - 4, DN_HEADS), F32)], axis=0)
        return jnp.broadcast_to(full.T.reshape(DN_HEADS * SUBLANES, 1), (DN_HEADS * SUBLANES, tm_inproj))

    gd = (w_in.shape[1] - off - 4 * DN_HEADS) // FN_GROUPS
    idx = jnp.arange(gd, dtype=jnp.int32)
    ang = ((idx[:, None] * idx[None, :]) % gd).astype(F32) * (2.0 * math.pi / gd)
    return dict(
        ffn1=(row(ffn1_norm),) + ffn_w(ffn1_w_gate, ffn1_w_up, ffn1_w_down),
        ffn2=(row(ffn2_norm),) + ffn_w(ffn2_w_gate, ffn2_w_up, ffn2_w_down),
        final=row(final_norm), mix=row(mix_norm), w_main=w_main, wgt=wgt,
        alog=gate_param(a_log), dt=gate_param(dt_bias),
        conv=jnp.pad(conv_w.astype(F32), ((0, SUBLANES - CONV_WIDTH), (0, 0))),
        hn=row(dn_head_norm), chan=(jnp.cos(ang), jnp.sin(ang)),
        wout_a=w_out[:dd].astype(BF16), wout_b=w_out[dd:].astype(BF16),
        xn=row(xattn_norm), mn=row(mem_norm), wq=xattn_w_q.astype(BF16), wkv=xattn_w_kv.astype(BF16),
        wo=xattn_w_o.astype(BF16))


def _gate_pairs(gt, batch, seq):
    nc = seq // CHUNK
    g = gt.reshape(DN_HEADS, SUBLANES, batch, nc, CHUNK)
    rev = lambda a: a[:, :, ::-1, :]
    pairs = jnp.stack([jnp.concatenate([g[:, 0], rev(g[:, 1])], axis=-1),
                       jnp.concatenate([g[:, 2], rev(g[:, 3])], axis=-1)], axis=1)
    pairs = jnp.pad(pairs, ((0, 0), (0, SUBLANES - 2), (0, 0), (0, 0), (0, 0)))
    return pairs.transpose(2, 3, 0, 1, 4).reshape(batch * nc, DN_HEADS * SUBLANES, 2 * CHUNK)


def _trunk(x, mem, w, tm_inproj, final):
    batch, seq, d = x.shape
    t = batch * seq
    dd = DN_HEADS * DN_HEAD_DIM
    x0 = x.reshape(t, d)
    x1 = _ffn(x0, *w["ffn1"], w["final"], final=False)
    p, gt = _inproj(x1, w["mix"], w["w_main"], w["wgt"], w["alog"], w["dt"], tm=tm_inproj)
    gp = _gate_pairs(gt, batch, seq)
    o_dn = _deltanet(p.reshape(batch, seq, -1), w["conv"], gp, w["hn"], batch=batch, seq=seq)
    dfn = p.shape[1] - 4 * dd
    scale = (seq * (dfn // FN_GROUPS)) ** -0.5
    wch = (jnp.concatenate(w["chan"], axis=1) * scale).astype(BF16)
    rc, rs = _chandft(p, wch, col_block=4 * dd // dfn)
    ctw, stw = _twiddle(seq)
    o_fn = _seqdft(ctw, stw, rc, rs, batch=batch, seq=seq)
    x2 = _outproj(x1, o_dn.reshape(t, dd), o_fn.reshape(t, dfn), w["wout_a"], w["wout_b"])
    kv = _kvproj(mem, w["mn"], w["wkv"])
    x3 = _xattn(x2, w["xn"], w["wq"], kv, w["wo"], batch=batch, seq=seq)
    y = _ffn(x3.reshape(t, d), *w["ffn2"], w["final"], final=final)
    return y.reshape(batch, seq, d)


def kernel(x_prompt, x_sample, mem_prompt, mem_sample, ffn1_norm, ffn1_w_gate, ffn1_w_up, ffn1_w_down, mix_norm,
           w_in, conv_w, a_log, dt_bias, dn_head_norm, w_out, xattn_norm, mem_norm, xattn_w_q, xattn_w_kv,
           xattn_w_o, ffn2_norm, ffn2_w_gate, ffn2_w_up, ffn2_w_down, final_norm):
    depth = ffn1_norm.shape[0]
    tm_inproj = 512
    y_prompt, y_sample = x_prompt, x_sample
    for l in range(depth):
        w = _prep_weights(ffn1_norm[l], ffn1_w_gate[l], ffn1_w_up[l], ffn1_w_down[l], mix_norm[l], w_in[l],
                          conv_w[l], a_log[l], dt_bias[l], dn_head_norm[l], w_out[l], xattn_norm[l], mem_norm[l],
                          xattn_w_q[l], xattn_w_kv[l], xattn_w_o[l], ffn2_norm[l], ffn2_w_gate[l], ffn2_w_up[l],
                          ffn2_w_down[l], final_norm, tm_inproj=tm_inproj)
        y_prompt = _trunk(y_prompt, mem_prompt, w, tm_inproj, l == depth - 1)
        y_sample = _trunk(y_sample, mem_sample, w, tm_inproj, l == depth - 1)
    return (y_prompt, y_sample)
```

```python
import functools
import math

import jax
import jax.numpy as jnp
from jax import lax
from jax.experimental import pallas as pl
from jax.experimental.pallas import tpu as pltpu

F32 = jnp.float32
BF16 = jnp.bfloat16
EPS = 1e-6

DN_HEADS = 8
DN_HEAD_DIM = 128
FN_GROUPS = 4
CONV_WIDTH = 5
CHUNK = 64
XA_HEADS = 4
N_MEM = 256

LANES = 128
SUBLANES = 8
VMEM_LIMIT = 56 * 1024 * 1024

NT_DIMS = (((1,), (1,)), ((), ()))
TN_DIMS = (((0,), (0,)), ((), ()))


def _params(*sem):
    return pltpu.CompilerParams(dimension_semantics=sem, vmem_limit_bytes=VMEM_LIMIT)


def _dot(a, b):
    return jnp.dot(a, b, preferred_element_type=F32)


def _rms(x, g):
    ms = jnp.mean(x * x, axis=-1, keepdims=True)
    return x * lax.rsqrt(ms + EPS) * g


def _sigmoid(x):
    return 1.0 / (1.0 + jnp.exp(-x))


def _ffn_body(x_ref, g_ref, wg_ref, wu_ref, wd_ref, fin_ref, o_ref, h_ref, acc_ref, *, final, last_width):
    j = pl.program_id(1)
    nj = pl.num_programs(1)
    tf = wg_ref.shape[1]

    @pl.when(j == 0)
    def _():
        h_ref[...] = _rms(x_ref[...], g_ref[...]).astype(BF16)
        acc_ref[...] = jnp.zeros_like(acc_ref)

    def hidden_block(width):
        h = h_ref[...]
        a = _dot(h, wg_ref[:, 0:width])
        u = _dot(h, wu_ref[:, 0:width])
        act = (a * _sigmoid(a) * u).astype(BF16)
        acc_ref[...] += _dot(act, wd_ref[0:width, :])

    if last_width == tf:
        hidden_block(tf)
    else:
        pl.when(j < nj - 1)(lambda: hidden_block(tf))
        pl.when(j == nj - 1)(lambda: hidden_block(last_width))

    @pl.when(j == nj - 1)
    def _():
        y = x_ref[...] + 0.5 * acc_ref[...]
        if final:
            y = _rms(y, fin_ref[...])
        o_ref[...] = y


def _ffn(x, g, wg, wu, wd, fin, *, final, tm=512, tf=1024):
    t, d = x.shape
    dff = wg.shape[1]
    nj = pl.cdiv(dff, tf)
    last_width = dff - (nj - 1) * tf
    assert last_width % LANES == 0, (dff, tf)
    return pl.pallas_call(
        functools.partial(_ffn_body, final=final, last_width=last_width),
        grid=(t // tm, nj),
        in_specs=[
            pl.BlockSpec((tm, d), lambda i, j: (i, 0)),
            pl.BlockSpec((1, d), lambda i, j: (0, 0)),
            pl.BlockSpec((d, tf), lambda i, j: (0, j)),
            pl.BlockSpec((d, tf), lambda i, j: (0, j)),
            pl.BlockSpec((tf, d), lambda i, j: (j, 0)),
            pl.BlockSpec((1, d), lambda i, j: (0, 0)),
        ],
        out_specs=pl.BlockSpec((tm, d), lambda i, j: (i, 0)),
        out_shape=jax.ShapeDtypeStruct((t, d), F32),
        scratch_shapes=[pltpu.VMEM((tm, d), BF16), pltpu.VMEM((tm, d), F32)],
        compiler_params=_params("parallel", "arbitrary"),
        name="ffn_final" if final else "ffn",
    )(x, g, wg, wu, wd, fin)


def _inproj_body(x_ref, g_ref, w_ref, wgt_ref, alog_ref, dt_ref, p_ref, gt_ref, h_ref):
    j = pl.program_id(1)

    @pl.when(j == 0)
    def _():
        h = _rms(x_ref[...], g_ref[...]).astype(BF16)
        h_ref[...] = h
        gt = lax.dot_general(wgt_ref[...], h, NT_DIMS, preferred_element_type=F32)
        kind = lax.broadcasted_iota(jnp.int32, gt.shape, 0) & (SUBLANES - 1)
        xs = gt + dt_ref[...]
        softplus = jnp.maximum(xs, 0.0) + jnp.log1p(jnp.exp(-jnp.abs(xs)))
        log_decay = -jnp.exp(alog_ref[...]) * softplus
        gt_ref[...] = jnp.where(kind < 2, _sigmoid(gt), jnp.where(kind < 4, log_decay, 0.0))

    p_ref[...] = _dot(h_ref[...], w_ref[...]).astype(BF16)


def _inproj(x, g, w, wgt, alog, dt, *, tm=512, tn=2560):
    t, d = x.shape
    n = w.shape[1]
    rows = wgt.shape[0]
    return pl.pallas_call(
        _inproj_body,
        grid=(t // tm, n // tn),
        in_specs=[
            pl.BlockSpec((tm, d), lambda i, j: (i, 0)),
            pl.BlockSpec((1, d), lambda i, j: (0, 0)),
            pl.BlockSpec((d, tn), lambda i, j: (0, j)),
            pl.BlockSpec((rows, d), lambda i, j: (0, 0)),
            pl.BlockSpec((rows, tm), lambda i, j: (0, 0)),
            pl.BlockSpec((rows, tm), lambda i, j: (0, 0)),
        ],
        out_specs=[
            pl.BlockSpec((tm, tn), lambda i, j: (i, j)),
            pl.BlockSpec((rows, tm), lambda i, j: (0, i)),
        ],
        out_shape=[jax.ShapeDtypeStruct((t, n), BF16), jax.ShapeDtypeStruct((rows, t), F32)],
        scratch_shapes=[pltpu.VMEM((tm, d), BF16)],
        compiler_params=_params("parallel", "arbitrary"),
        name="inproj",
    )(x, g, w, wgt, alog, dt)


CONV_ROWS = 512
CONV_PIECE = 128
HALO = SUBLANES
PAIR = 2 * CHUNK


def _split2(x):
    hi = x.astype(BF16)
    lo = (x - hi.astype(F32)).astype(BF16)
    return hi, lo


def _aligned(x, m):
    return x if isinstance(x, int) else pl.multiple_of(x, m)


def _advance(gen):
    try:
        next(gen)
        return True
    except StopIteration:
        return False


PREP_STAGES = 14


def _deltanet_body(q_ref, k_ref, v_ref, z_ref, cwq_ref, cwk_ref, cwv_ref, gp_ref, hn_ref, o_ref,
                   upq, upk, upv, qs, ks, vs, pq_f, pq_b, n_f, n_b, oc_f, oc_b, el_scr, of, ob, bd1,
                   *, seq, unroll):
    nc = seq // CHUNK
    c = CHUNK
    nblk = nc // unroll

    streams = ((q_ref, cwq_ref, upq, qs, DN_HEAD_DIM ** -0.5), (k_ref, cwk_ref, upk, ks, 1.0),
               (v_ref, cwv_ref, upv, vs, None))
    for src_ref, _, up, _, _ in streams:
        zeros = jnp.zeros((HALO, LANES), F32)
        up[0:HALO, :] = zeros
        up[seq + HALO:seq + 2 * HALO, :] = zeros

        def fill(r, _, src_ref=src_ref, up=up):
            r0 = pl.multiple_of(r * CONV_ROWS, CONV_ROWS)
            up[pl.ds(r0 + HALO, CONV_ROWS), :] = src_ref[pl.ds(r0, CONV_ROWS), :].astype(F32)
            return 0

        lax.fori_loop(0, seq // CONV_ROWS, fill, 0)

    def conv_block(r0, cw_ref, up, dst_ref, scale):
        cw = cw_ref[...]
        for s0 in range(r0, r0 + CONV_ROWS, CONV_PIECE):
            y = None
            for tap in range(CONV_WIDTH):
                term = cw[tap:tap + 1, :] * up[pl.ds(s0 + (HALO + tap - CONV_WIDTH // 2), CONV_PIECE), :]
                y = term if y is None else y + term
            y = y / (1.0 + jnp.exp(-y))
            if scale is not None:
                y = y * lax.rsqrt(jnp.sum(y * y, axis=-1, keepdims=True) + EPS)
                if scale != 1.0:
                    y = y * scale
            dst_ref[pl.ds(s0, CONV_PIECE), :] = y

    def conv_rows(rows):
        for r0 in rows:
            for _, cw_ref, up, dst_ref, scale in streams:
                conv_block(r0, cw_ref, up, dst_ref, scale)
                yield

    edge = unroll * c
    all_rows = range(0, seq, CONV_ROWS)
    edge_rows = [r for r in all_rows if r < edge or r + CONV_ROWS > seq - edge]
    mid_rows = [r for r in all_rows if r not in edge_rows]
    for _ in conv_rows(edge_rows):
        pass

    ii = lax.broadcasted_iota(jnp.int32, (c, PAIR), 0)
    ll = lax.broadcasted_iota(jnp.int32, (c, PAIR), 1)
    jj = ll & (c - 1)
    is_f = ll < c
    ahead = jnp.where(is_f, ii - jj, jj - ii)
    incl = ahead >= 0
    strict = ahead > 0
    eye = ii == jj
    xor = ii ^ jj

    rr = lax.broadcasted_iota(jnp.int32, (2 * PAIR, 2 * PAIR), 0)
    nn = lax.broadcasted_iota(jnp.int32, (2 * PAIR, 2 * PAIR), 1)
    bd1[...] = jnp.where(((rr & (PAIR - 1)) >> (c.bit_length() - 1)) == (nn >> (PAIR.bit_length() - 1)),
                         1.0, 0.0).astype(BF16)

    def blockdiag(xp):
        top = jnp.where(is_f, xp, 0.0)
        bot = jnp.where(is_f, 0.0, xp)
        return jnp.concatenate([top, bot], axis=0).astype(BF16)

    is_b = ll >= c
    is_f2 = lax.broadcasted_iota(jnp.int32, (2 * c, PAIR), 1) < c
    zer = jnp.zeros((c, LANES), BF16)

    def prep_pair(i, slot):
        ra = _aligned(i * c, c)
        rb = _aligned((nc - 1 - i) * c, c)
        ka = ks[pl.ds(ra, c), :]
        qa = qs[pl.ds(ra, c), :]
        kz = ks[pl.ds(rb, c), :]
        qz = qs[pl.ds(rb, c), :]
        ka16, qa16, kz16, qz16 = ka.astype(BF16), qa.astype(BF16), kz.astype(BF16), qz.astype(BF16)
        va16 = vs[pl.ds(ra, c), :].astype(BF16)
        vz16 = vs[pl.ds(rb, c), :].astype(BF16)
        gram_a = lax.dot_general(jnp.concatenate([ka16, qa16], axis=0), jnp.concatenate([ka16, ka16], axis=0),
                                 NT_DIMS, preferred_element_type=F32)
        gram_z = lax.dot_general(jnp.concatenate([kz16, qz16], axis=0), jnp.concatenate([kz16, kz16], axis=0),
                                 NT_DIMS, preferred_element_type=F32)
        tile = gp_ref[i]
        brow = tile[0:1, :]
        grow = tile[1:2, :]
        x = jnp.concatenate([jnp.where(incl, grow, 0.0), jnp.where(eye, brow, 0.0)], axis=0)
        hi, lo = _split2(x)
        cb = _dot(jnp.concatenate([hi, lo], axis=1), bd1[...])
        yield
        gram = jnp.where(is_f2, gram_a, gram_z)
        cf_b = cb[0:c, 0:PAIR]
        cb_b = cb[0:c, PAIR:2 * PAIR]
        cpair = jnp.where(is_f, cf_b, cb_b)
        bpair = jnp.where(is_f, cb[c:2 * c, 0:PAIR], cb[c:2 * c, PAIR:2 * PAIR])
        crow = jnp.sum(jnp.where(eye, cpair, 0.0), axis=0, keepdims=True)
        decay = jnp.where(incl, jnp.exp(jnp.where(incl, cpair - crow, 0.0)), 0.0)
        a = jnp.where(strict, bpair * gram[0:c, :] * decay, 0.0)
        attn16 = (gram[c:2 * c, :] * decay).astype(BF16)

        t = jnp.where(eye, 1.0, 0.0) - jnp.where(xor == 1, a, 0.0)
        for lvl in range(1, 6):
            e = jnp.where((xor >> lvl) == 1, a, 0.0)
            te = _dot(t.astype(BF16), blockdiag(e))
            yield
            tet = _dot(te.astype(BF16), blockdiag(t))
            yield
            t = t - tet

        erow = jnp.exp(crow)
        bdv = jnp.concatenate([jnp.concatenate([va16, zer], axis=1), jnp.concatenate([zer, vz16], axis=1)], axis=0)
        bdk = jnp.concatenate([jnp.concatenate([ka16, zer], axis=1), jnp.concatenate([zer, kz16], axis=1)], axis=0)
        u = _dot((t * brow).astype(BF16), bdv)
        w = _dot((t * (brow * erow)).astype(BF16), bdk)
        yield
        last_f = cf_b[c - 1:c, :]
        last_b = cb_b[0:1, :]

        p0 = _aligned(slot * (3 * c), c)
        n0 = _aligned(slot * (2 * c), 2 * c)
        o0 = _aligned(slot * c, c)
        dirs = ((0, ka, qa, cf_b, last_f, is_f, pq_f, n_f, oc_f),
                (LANES, kz, qz, cb_b, last_b, is_b, pq_b, n_b, oc_b))
        prods = []
        for (lo, kx, qx, cx, last, keep, pq, nn_ref, oc) in dirs:
            uw = jnp.concatenate([u[:, lo:lo + LANES], w[:, lo:lo + LANES]], axis=1).astype(BF16)
            kdec = (kx * jnp.exp(last - cx)).astype(BF16)
            n_p = lax.dot_general(kdec, uw, TN_DIMS, preferred_element_type=F32)
            o_aw = _dot(jnp.where(keep, attn16, zer), jnp.concatenate([uw, uw], axis=0))
            prods.append((n_p, o_aw))
        yield
        for (lo, kx, qx, cx, last, keep, pq, nn_ref, oc), (n_p, o_aw) in zip(dirs, prods):
            pq[pl.ds(p0, 2 * c), :] = n_p[:, LANES:2 * LANES].astype(BF16)
            pq[pl.ds(p0 + 2 * c, c), :] = (qx * jnp.exp(cx) - o_aw[:, LANES:2 * LANES]).astype(BF16)
            nn_ref[pl.ds(n0, 2 * c), :] = n_p[:, 0:LANES]
            oc[pl.ds(o0, c), :] = o_aw[:, 0:LANES]
        el_scr[slot] = jnp.concatenate([jnp.exp(last_f), jnp.exp(last_b),
                                        jnp.zeros((SUBLANES - 2, LANES), F32)], axis=0)

    def rec_block(blk, state):
        s_f, s_b = state
        for j in range(unroll):
            i = blk * unroll + j
            slot = (blk & 1) * unroll + j
            p0 = _aligned(slot * (3 * c), c)
            n0 = _aligned(slot * (2 * c), 2 * c)
            o0 = _aligned(slot * c, c)
            el = el_scr[slot]
            r_f = _dot(pq_f[pl.ds(p0, 3 * c), :], s_f.astype(BF16))
            r_b = _dot(pq_b[pl.ds(p0, 3 * c), :], s_b.astype(BF16))
            yield
            of[pl.ds(_aligned(i * c, c), c), :] = oc_f[pl.ds(o0, c), :] + r_f[2 * c:3 * c, :]
            s_f = s_f * el[0:1, :] + n_f[pl.ds(n0, 2 * c), :] - r_f[0:2 * c, :]
            ob[pl.ds(_aligned((nc - 1 - i) * c, c), c), :] = oc_b[pl.ds(o0, c), :] + r_b[2 * c:3 * c, :]
            s_b = s_b * el[1:2, :] + n_b[pl.ds(n0, 2 * c), :] - r_b[0:2 * c, :]
        state[0], state[1] = s_f, s_b

    def run_interleaved(gens, rec=None):
        rec_period = max(1, (PREP_STAGES * unroll) // (unroll + 1))
        steps = 0
        while gens or rec is not None:
            alive = []
            for g in gens:
                if _advance(g):
                    alive.append(g)
                steps += 1
                if rec is not None and steps % rec_period == 0 and not _advance(rec):
                    rec = None
            gens = alive
            if not gens and rec is not None and not _advance(rec):
                rec = None

    def prep_gens(blk):
        return [prep_pair(blk * unroll + j, (blk & 1) * unroll + j) for j in range(unroll)]

    def pipelined(blk, carry):
        state = list(carry)
        run_interleaved(prep_gens(blk), rec_block(blk - 1, state))
        return tuple(state)

    def finish_rows(rows):
        hn = hn_ref[...]
        for r0 in rows:
            for s0 in range(r0, r0 + CONV_ROWS, CONV_PIECE):
                o = of[pl.ds(s0, CONV_PIECE), :] + ob[pl.ds(s0, CONV_PIECE), :]
                zz = z_ref[pl.ds(s0, CONV_PIECE), :].astype(F32)
                o_ref[pl.ds(s0, CONV_PIECE), :] = (_rms(o, hn) * (zz / (1.0 + jnp.exp(-zz)))).astype(BF16)
            yield

    run_interleaved(prep_gens(0) + [conv_rows(mid_rows)])
    s0 = jnp.zeros((LANES, LANES), F32)
    state = list(lax.fori_loop(1, nblk, pipelined, (s0, s0)))
    run_interleaved([finish_rows(mid_rows)] if nblk > 1 else [], rec_block(nblk - 1, state))
    for _ in finish_rows(edge_rows if nblk > 1 else all_rows):
        pass


DN_UNROLL = 16
DN_MIN_BLOCKS = 4


def _deltanet(p, cw, gp, hn, *, batch, seq):
    assert seq % CONV_ROWS == 0 and CONV_ROWS % CHUNK == 0, (seq, CONV_ROWS)
    nc = seq // CHUNK
    unroll = math.gcd(nc, DN_UNROLL)
    while unroll > 1 and nc // unroll < DN_MIN_BLOCKS:
        unroll //= 2
    h = DN_HEADS
    slots = 2 * unroll
    c = CHUNK
    col = lambda off: pl.BlockSpec((None, seq, LANES), lambda b, hd, off=off: (b, 0, off + hd))
    cws = lambda off: pl.BlockSpec((SUBLANES, LANES), lambda b, hd, off=off: (0, off + hd))
    return pl.pallas_call(
        functools.partial(_deltanet_body, seq=seq, unroll=unroll),
        grid=(batch, h),
        in_specs=[col(0), col(h), col(2 * h), col(3 * h), cws(0), cws(h), cws(2 * h),
                  pl.BlockSpec((nc, SUBLANES, LANES), lambda b, hd: (b, hd, 0)),
                  pl.BlockSpec((1, LANES), lambda b, hd: (0, 0))],
        out_specs=pl.BlockSpec((None, seq, LANES), lambda b, hd: (b, 0, hd)),
        out_shape=jax.ShapeDtypeStruct((batch, seq, h * LANES), BF16),
        scratch_shapes=[
            pltpu.VMEM((seq + 2 * HALO, LANES), F32),
            pltpu.VMEM((seq + 2 * HALO, LANES), F32),
            pltpu.VMEM((seq + 2 * HALO, LANES), F32),
            pltpu.VMEM((seq, LANES), F32),
            pltpu.VMEM((seq, LANES), F32),
            pltpu.VMEM((seq, LANES), F32),
            pltpu.VMEM((slots * 3 * c, LANES), BF16),
            pltpu.VMEM((slots * 3 * c, LANES), BF16),
            pltpu.VMEM((slots * 2 * c, LANES), F32),
            pltpu.VMEM((slots * 2 * c, LANES), F32),
            pltpu.VMEM((slots * c, LANES), F32),
            pltpu.VMEM((slots * c, LANES), F32),
            pltpu.VMEM((slots, SUBLANES, LANES), F32),
            pltpu.VMEM((seq, LANES), F32),
            pltpu.VMEM((seq, LANES), F32),
            pltpu.VMEM((2 * PAIR, 2 * PAIR), BF16),
        ],
        compiler_params=_params("parallel", "parallel"),
        name="deltanet",
    )(p, p, p, p, cw, cw, cw, gp, hn)


TW_ROWS = 64


def _twiddle_body(ac_ref, as_ref, bc_ref, bs_ref, c_ref, s_ref):
    ac = ac_ref[...]
    sn = as_ref[...]
    bc = bc_ref[...]
    bs = bs_ref[...]
    c_ref[...] = (ac * bc - sn * bs).astype(BF16)
    s_ref[...] = (-(sn * bc + ac * bs)).astype(BF16)


def _twiddle(seq):
    nblk = seq // TW_ROWS
    kk = jnp.arange(seq, dtype=jnp.int32)[None, :]
    step = 2.0 * math.pi / seq
    ang_a = ((jnp.arange(TW_ROWS, dtype=jnp.int32)[:, None] * kk) % seq).astype(F32) * step
    ang_b = ((jnp.arange(nblk, dtype=jnp.int32)[:, None] * TW_ROWS * kk) % seq).astype(F32) * step
    tab = pl.BlockSpec((TW_ROWS, seq), lambda i: (0, 0))
    row = pl.BlockSpec((None, 1, seq), lambda i: (i, 0, 0))
    out = pl.BlockSpec((TW_ROWS, seq), lambda i: (i, 0))
    return pl.pallas_call(
        _twiddle_body,
        grid=(nblk,),
        in_specs=[tab, tab, row, row],
        out_specs=[out, out],
        out_shape=[jax.ShapeDtypeStruct((seq, seq), BF16)] * 2,
        compiler_params=_params("parallel"),
        name="twiddle",
    )(jnp.cos(ang_a), jnp.sin(ang_a), jnp.cos(ang_b)[:, None, :], jnp.sin(ang_b)[:, None, :])


def _chandft_body(u_ref, w_ref, rc_ref, rs_ref):
    gd = w_ref.shape[0]
    w = w_ref[...]
    for g in range(FN_GROUPS):
        y = _dot(u_ref[:, g * gd:(g + 1) * gd], w)
        rc_ref[:, g * gd:(g + 1) * gd] = y[:, 0:gd].astype(BF16)
        rs_ref[:, g * gd:(g + 1) * gd] = y[:, gd:2 * gd].astype(BF16)


def _chandft(p, wch, *, col_block, tm=512):
    t = p.shape[0]
    dfn = FN_GROUPS * wch.shape[0]
    out = pl.BlockSpec((tm, dfn), lambda i: (i, 0))
    return pl.pallas_call(
        _chandft_body,
        grid=(t // tm,),
        in_specs=[pl.BlockSpec((tm, dfn), lambda i: (i, col_block)),
                  pl.BlockSpec(wch.shape, lambda i: (0, 0))],
        out_specs=[out, out],
        out_shape=[jax.ShapeDtypeStruct((t, dfn), BF16)] * 2,
        compiler_params=_params("parallel"),
        name="chandft",
    )(p, wch)


def _seqdft_body(c_ref, s_ref, rc_ref, rs_ref, o_ref):
    o_ref[...] = (_dot(c_ref[...], rc_ref[...]) + _dot(s_ref[...], rs_ref[...])).astype(BF16)


def _seqdft(ctw, stw, rc, rs, *, batch, seq, tm=256):
    dfn = rc.shape[-1]
    tw = pl.BlockSpec((tm, seq), lambda b, i: (i, 0))
    rhs = pl.BlockSpec((None, seq, dfn), lambda b, i: (b, 0, 0))
    return pl.pallas_call(
        _seqdft_body,
        grid=(batch, seq // tm),
        in_specs=[tw, tw, rhs, rhs],
        out_specs=pl.BlockSpec((None, tm, dfn), lambda b, i: (b, i, 0)),
        out_shape=jax.ShapeDtypeStruct((batch, seq, dfn), BF16),
        compiler_params=_params("parallel", "arbitrary"),
        name="seqdft",
    )(ctw, stw, rc.reshape(batch, seq, dfn), rs.reshape(batch, seq, dfn))


def _outproj_body(x_ref, a_ref, b_ref, wa_ref, wb_ref, o_ref):
    o_ref[...] = x_ref[...] + _dot(a_ref[...], wa_ref[...]) + _dot(b_ref[...], wb_ref[...])


def _outproj(x, a, b, wa, wb, *, tm=512):
    t, d = x.shape
    da = a.shape[1]
    db = b.shape[1]
    return pl.pallas_call(
        _outproj_body,
        grid=(t // tm,),
        in_specs=[pl.BlockSpec((tm, d), lambda i: (i, 0)),
                  pl.BlockSpec((tm, da), lambda i: (i, 0)),
                  pl.BlockSpec((tm, db), lambda i: (i, 0)),
                  pl.BlockSpec((da, d), lambda i: (0, 0)),
                  pl.BlockSpec((db, d), lambda i: (0, 0))],
        out_specs=pl.BlockSpec((tm, d), lambda i: (i, 0)),
        out_shape=jax.ShapeDtypeStruct((t, d), F32),
        compiler_params=_params("parallel"),
        name="outproj",
    )(x, a, b, wa, wb)


def _kvproj_body(m_ref, g_ref, w_ref, o_ref, h_ref):
    @pl.when(pl.program_id(1) == 0)
    def _():
        h_ref[...] = _rms(m_ref[...], g_ref[...]).astype(BF16)

    o_ref[...] = _dot(h_ref[...], w_ref[...]).astype(BF16)


def _kvproj(mem, g, w, *, tn=1024):
    b, n, d = mem.shape
    nout = w.shape[1]
    return pl.pallas_call(
        _kvproj_body,
        grid=(b, nout // tn),
        in_specs=[pl.BlockSpec((None, n, d), lambda i, j: (i, 0, 0)),
                  pl.BlockSpec((1, d), lambda i, j: (0, 0)),
                  pl.BlockSpec((d, tn), lambda i, j: (0, j))],
        out_specs=pl.BlockSpec((None, n, tn), lambda i, j: (i, 0, j)),
        out_shape=jax.ShapeDtypeStruct((b, n, nout), BF16),
        scratch_shapes=[pltpu.VMEM((n, d), BF16)],
        compiler_params=_params("parallel", "arbitrary"),
        name="kvproj",
    )(mem, g, w)


def _xattn_body(x_ref, g_ref, wq_ref, kv_ref, wo_ref, o_ref, q_scr, a_scr):
    d = x_ref.shape[-1]
    hd = d // XA_HEADS
    x = x_ref[...]
    q_scr[...] = _dot(_rms(x, g_ref[...]).astype(BF16), wq_ref[...]).astype(BF16)
    for h in range(XA_HEADS):
        kh = kv_ref[:, h * hd:(h + 1) * hd]
        vh = kv_ref[:, d + h * hd:d + (h + 1) * hd]
        s = lax.dot_general(q_scr[:, h * hd:(h + 1) * hd], kh, NT_DIMS,
                            preferred_element_type=F32) * (hd ** -0.5)
        p = jnp.exp(s - jnp.max(s, axis=-1, keepdims=True))
        l = jnp.sum(p, axis=-1, keepdims=True)
        a_scr[:, h * hd:(h + 1) * hd] = (_dot(p.astype(BF16), vh) / l).astype(BF16)
    o_ref[...] = x + _dot(a_scr[...], wo_ref[...])


def _xattn(x, g, wq, kv, wo, *, batch, seq, tm=256):
    d = x.shape[-1]
    n = kv.shape[1]
    return pl.pallas_call(
        _xattn_body,
        grid=(batch, seq // tm),
        in_specs=[pl.BlockSpec((None, tm, d), lambda b, i: (b, i, 0)),
                  pl.BlockSpec((1, d), lambda b, i: (0, 0)),
                  pl.BlockSpec((d, d), lambda b, i: (0, 0)),
                  pl.BlockSpec((None, n, 2 * d), lambda b, i: (b, 0, 0)),
                  pl.BlockSpec((d, d), lambda b, i: (0, 0))],
        out_specs=pl.BlockSpec((None, tm, d), lambda b, i: (b, i, 0)),
        out_shape=jax.ShapeDtypeStruct((batch, seq, d), F32),
        scratch_shapes=[pltpu.VMEM((tm, d), BF16), pltpu.VMEM((tm, d), BF16)],
        compiler_params=_params("parallel", "arbitrary"),
        name="xattn",
    )(x.reshape(batch, seq, d), g, wq, kv, wo)


def _prep_weights(ffn1_norm, ffn1_w_gate, ffn1_w_up, ffn1_w_down, mix_norm, w_in, conv_w, a_log, dt_bias,
                  dn_head_norm, w_out, xattn_norm, mem_norm, xattn_w_q, xattn_w_kv, xattn_w_o, ffn2_norm,
                  ffn2_w_gate, ffn2_w_up, ffn2_w_down, final_norm, *, tm_inproj):
    d = w_in.shape[0]
    dd = DN_HEADS * DN_HEAD_DIM

    def ffn_w(wg, wu, wd):
        return (wg.astype(BF16), wu.astype(BF16), wd.astype(BF16))

    row = lambda v: v.reshape(1, -1).astype(F32)
    off = 4 * dd
    w_main = jnp.concatenate([w_in[:, :off], w_in[:, off + 4 * DN_HEADS:]], axis=1).astype(BF16)
    beta_w = w_in[:, off:off + 2 * DN_HEADS].reshape(d, 2, DN_HEADS)
    a_w = w_in[:, off + 2 * DN_HEADS:off + 4 * DN_HEADS].reshape(d, 2, DN_HEADS)
    gate_w = jnp.concatenate([beta_w, a_w, jnp.zeros((d, SUBLANES - 4, DN_HEADS), F32)], axis=1)
    wgt = gate_w.transpose(2, 1, 0).reshape(DN_HEADS * SUBLANES, d).astype(BF16)

    def gate_param(v):
        full = jnp.concatenate([jnp.zeros((2, DN_HEADS), F32), v.astype(F32),
                                jnp.zeros((SUBLANES - 4, DN_HEADS), F32)], axis=0)
        return jnp.broadcast_to(full.T.reshape(DN_HEADS * SUBLANES, 1), (DN_HEADS * SUBLANES, tm_inproj))

    gd = (w_in.shape[1] - off - 4 * DN_HEADS) // FN_GROUPS
    idx = jnp.arange(gd, dtype=jnp.int32)
    ang = ((idx[:, None] * idx[None, :]) % gd).astype(F32) * (2.0 * math.pi / gd)
    return dict(
        ffn1=(row(ffn1_norm),) + ffn_w(ffn1_w_gate, ffn1_w_up, ffn1_w_down),
        ffn2=(row(ffn2_norm),) + ffn_w(ffn2_w_gate, ffn2_w_up, ffn2_w_down),
        final=row(final_norm), mix=row(mix_norm), w_main=w_main, wgt=wgt,
        alog=gate_param(a_log), dt=gate_param(dt_bias),
        conv=jnp.pad(conv_w.astype(F32), ((0, SUBLANES - CONV_WIDTH), (0, 0))),
        hn=row(dn_head_norm), chan=(jnp.cos(ang), jnp.sin(ang)),
        wout_a=w_out[:dd].astype(BF16), wout_b=w_out[dd:].astype(BF16),
        xn=row(xattn_norm), mn=row(mem_norm), wq=xattn_w_q.astype(BF16), wkv=xattn_w_kv.astype(BF16),
        wo=xattn_w_o.astype(BF16))


def _gate_pairs(gt, batch, seq):
    nc = seq // CHUNK
    g = gt.reshape(DN_HEADS, SUBLANES, batch, nc, CHUNK)
    rev = lambda a: a[:, :, ::-1, :]
    pairs = jnp.stack([jnp.concatenate([g[:, 0], rev(g[:, 1])], axis=-1),
                       jnp.concatenate([g[:, 2], rev(g[:, 3])], axis=-1)], axis=1)
    pairs = jnp.pad(pairs, ((0, 0), (0, SUBLANES - 2), (0, 0), (0, 0), (0, 0)))
    return pairs.transpose(2, 3, 0, 1, 4).reshape(batch * nc, DN_HEADS * SUBLANES, 2 * CHUNK)


def _trunk(x, mem, w, tm_inproj, final):
    batch, seq, d = x.shape
    t = batch * seq
    dd = DN_HEADS * DN_HEAD_DIM
    x0 = x.reshape(t, d)
    x1 = _ffn(x0, *w["ffn1"], w["final"], final=False)
    p, gt = _inproj(x1, w["mix"], w["w_main"], w["wgt"], w["alog"], w["dt"], tm=tm_inproj)
    gp = _gate_pairs(gt, batch, seq)
    o_dn = _deltanet(p.reshape(batch, seq, -1), w["conv"], gp, w["hn"], batch=batch, seq=seq)
    dfn = p.shape[1] - 4 * dd
    scale = (seq * (dfn // FN_GROUPS)) ** -0.5
    wch = (jnp.concatenate(w["chan"], axis=1) * scale).astype(BF16)
    rc, rs = _chandft(p, wch, col_block=4 * dd // dfn)
    ctw, stw = _twiddle(seq)
    o_fn = _seqdft(ctw, stw, rc, rs, batch=batch, seq=seq)
    x2 = _outproj(x1, o_dn.reshape(t, dd), o_fn.reshape(t, dfn), w["wout_a"], w["wout_b"])
    kv = _kvproj(mem, w["mn"], w["wkv"])
    x3 = _xattn(x2, w["xn"], w["wq"], kv, w["wo"], batch=batch, seq=seq)
    y = _ffn(x3.reshape(t, d), *w["ffn2"], w["final"], final=final)
    return y.reshape(batch, seq, d)


def kernel(x_prompt, x_sample, mem_prompt, mem_sample, ffn1_norm, ffn1_w_gate, ffn1_w_up, ffn1_w_down, mix_norm,
           w_in, conv_w, a_log, dt_bias, dn_head_norm, w_out, xattn_norm, mem_norm, xattn_w_q, xattn_w_kv,
           xattn_w_o, ffn2_norm, ffn2_w_gate, ffn2_w_up, ffn2_w_down, final_norm):
    depth = ffn1_norm.shape[0]
    tm_inproj = 512
    y_prompt, y_sample = x_prompt, x_sample
    for l in range(depth):
        w = _prep_weights(ffn1_norm[l], ffn1_w_gate[l], ffn1_w_up[l], ffn1_w_down[l], mix_norm[l], w_in[l],
                          conv_w[l], a_log[l], dt_bias[l], dn_head_norm[l], w_out[l], xattn_norm[l], mem_norm[l],
                          xattn_w_q[l], xattn_w_kv[l], xattn_w_o[l], ffn2_norm[l], ffn2_w_gate[l], ffn2_w_up[l],
                          ffn2_w_down[l], final_norm, tm_inproj=tm_inproj)
        y_prompt = _trunk(y_prompt, mem_prompt, w, tm_inproj, l == depth - 1)
        y_sample = _trunk(y_sample, mem_sample, w, tm_inproj, l == depth - 1)
    return (y_prompt, y_sample)
```

```python
import functools
import math

import jax
import jax.numpy as jnp
from jax import lax
from jax.experimental import pallas as pl
from jax.experimental.pallas import tpu as pltpu

F32 = jnp.float32
BF16 = jnp.bfloat16
EPS = 1e-6

DN_HEADS = 8
DN_HEAD_DIM = 128
FN_GROUPS = 4
CONV_WIDTH = 5
CHUNK = 64
XA_HEADS = 4
N_MEM = 256

LANES = 128
SUBLANES = 8
VMEM_LIMIT = 56 * 1024 * 1024

NT_DIMS = (((1,), (1,)), ((), ()))
TN_DIMS = (((0,), (0,)), ((), ()))


def _params(*sem):
    return pltpu.CompilerParams(dimension_semantics=sem, vmem_limit_bytes=VMEM_LIMIT)


def _dot(a, b):
    return jnp.dot(a, b, preferred_element_type=F32)


def _rms(x, g):
    ms = jnp.mean(x * x, axis=-1, keepdims=True)
    return x * lax.rsqrt(ms + EPS) * g


def _sigmoid(x):
    return 1.0 / (1.0 + jnp.exp(-x))


def _ffn_body(x_ref, g_ref, wg_ref, wu_ref, wd_ref, fin_ref, o_ref, h_ref, *, final, last_width):
    j = pl.program_id(1)
    nj = pl.num_programs(1)
    tf = wg_ref.shape[1]

    @pl.when(j == 0)
    def _():
        h_ref[...] = _rms(x_ref[...], g_ref[...]).astype(BF16)
        o_ref[...] = jnp.zeros_like(o_ref)

    def hidden_block(width):
        h = h_ref[...]
        a = _dot(h, wg_ref[:, 0:width])
        u = _dot(h, wu_ref[:, 0:width])
        act = (a * _sigmoid(a) * u).astype(BF16)
        o_ref[...] += _dot(act, wd_ref[0:width, :])

    if last_width == tf:
        hidden_block(tf)
    else:
        pl.when(j < nj - 1)(lambda: hidden_block(tf))
        pl.when(j == nj - 1)(lambda: hidden_block(last_width))

    @pl.when(j == nj - 1)
    def _():
        y = x_ref[...] + 0.5 * o_ref[...]
        if final:
            y = _rms(y, fin_ref[...])
        o_ref[...] = y


def _ffn(x, g, wg, wu, wd, fin, *, final, tm=1024, tf=512):
    t, d = x.shape
    tm = min(tm, t)
    assert t % tm == 0, (t, tm)
    dff = wg.shape[1]
    nj = pl.cdiv(dff, tf)
    last_width = dff - (nj - 1) * tf
    assert last_width % LANES == 0, (dff, tf)
    return pl.pallas_call(
        functools.partial(_ffn_body, final=final, last_width=last_width),
        grid=(t // tm, nj),
        in_specs=[
            pl.BlockSpec((tm, d), lambda i, j: (i, 0)),
            pl.BlockSpec((1, d), lambda i, j: (0, 0)),
            pl.BlockSpec((d, tf), lambda i, j: (0, j)),
            pl.BlockSpec((d, tf), lambda i, j: (0, j)),
            pl.BlockSpec((tf, d), lambda i, j: (j, 0)),
            pl.BlockSpec((1, d), lambda i, j: (0, 0)),
        ],
        out_specs=pl.BlockSpec((tm, d), lambda i, j: (i, 0)),
        out_shape=jax.ShapeDtypeStruct((t, d), F32),
        scratch_shapes=[pltpu.VMEM((tm, d), BF16)],
        compiler_params=_params("parallel", "arbitrary"),
        name="ffn_final" if final else "ffn",
    )(x, g, wg, wu, wd, fin)


def _inproj_body(x_ref, g_ref, w_ref, wgt_ref, alog_ref, dt_ref, p_ref, gt_ref, h_ref):
    j = pl.program_id(1)

    @pl.when(j == 0)
    def _():
        h = _rms(x_ref[...], g_ref[...]).astype(BF16)
        h_ref[...] = h
        gt = lax.dot_general(wgt_ref[...], h, NT_DIMS, preferred_element_type=F32)
        kind = lax.broadcasted_iota(jnp.int32, gt.shape, 0) & (SUBLANES - 1)
        xs = gt + dt_ref[...]
        softplus = jnp.maximum(xs, 0.0) + jnp.log1p(jnp.exp(-jnp.abs(xs)))
        log_decay = -jnp.exp(alog_ref[...]) * softplus
        gt_ref[...] = jnp.where(kind < 2, _sigmoid(gt), jnp.where(kind < 4, log_decay, 0.0))

    p_ref[...] = _dot(h_ref[...], w_ref[...]).astype(BF16)


def _inproj(x, g, w, wgt, alog, dt, *, tm=512, tn=2560):
    t, d = x.shape
    n = w.shape[1]
    rows = wgt.shape[0]
    return pl.pallas_call(
        _inproj_body,
        grid=(t // tm, n // tn),
        in_specs=[
            pl.BlockSpec((tm, d), lambda i, j: (i, 0)),
            pl.BlockSpec((1, d), lambda i, j: (0, 0)),
            pl.BlockSpec((d, tn), lambda i, j: (0, j)),
            pl.BlockSpec((rows, d), lambda i, j: (0, 0)),
            pl.BlockSpec((rows, tm), lambda i, j: (0, 0)),
            pl.BlockSpec((rows, tm), lambda i, j: (0, 0)),
        ],
        out_specs=[
            pl.BlockSpec((tm, tn), lambda i, j: (i, j)),
            pl.BlockSpec((rows, tm), lambda i, j: (0, i)),
        ],
        out_shape=[jax.ShapeDtypeStruct((t, n), BF16), jax.ShapeDtypeStruct((rows, t), F32)],
        scratch_shapes=[pltpu.VMEM((tm, d), BF16)],
        compiler_params=_params("parallel", "arbitrary"),
        name="inproj",
    )(x, g, w, wgt, alog, dt)


CONV_ROWS = 512
CONV_PIECE = 128
HALO = SUBLANES
PAIR = 2 * CHUNK


def _split2(x):
    hi = x.astype(BF16)
    lo = (x - hi.astype(F32)).astype(BF16)
    return hi, lo


def _aligned(x, m):
    return x if isinstance(x, int) else pl.multiple_of(x, m)


def _advance(gen):
    try:
        next(gen)
        return True
    except StopIteration:
        return False


PREP_STAGES = 14


def _deltanet_body(q_ref, k_ref, v_ref, z_ref, cwq_ref, cwk_ref, cwv_ref, gp_ref, hn_ref, o_ref,
                   upq, upk, upv, qs, ks, vs, pq_f, pq_b, n_f, n_b, oc_f, oc_b, el_scr, of, ob, bd1,
                   *, seq, unroll):
    nc = seq // CHUNK
    c = CHUNK
    nblk = nc // unroll

    streams = ((q_ref, cwq_ref, upq, qs, DN_HEAD_DIM ** -0.5), (k_ref, cwk_ref, upk, ks, 1.0),
               (v_ref, cwv_ref, upv, vs, None))
    for src_ref, _, up, _, _ in streams:
        zeros = jnp.zeros((HALO, LANES), F32)
        up[0:HALO, :] = zeros
        up[seq + HALO:seq + 2 * HALO, :] = zeros

        def fill(r, _, src_ref=src_ref, up=up):
            r0 = pl.multiple_of(r * CONV_ROWS, CONV_ROWS)
            up[pl.ds(r0 + HALO, CONV_ROWS), :] = src_ref[pl.ds(r0, CONV_ROWS), :].astype(F32)
            return 0

        lax.fori_loop(0, seq // CONV_ROWS, fill, 0)

    def conv_block(r0, cw_ref, up, dst_ref, scale):
        cw = cw_ref[...]
        for s0 in range(r0, r0 + CONV_ROWS, CONV_PIECE):
            y = None
            for tap in range(CONV_WIDTH):
                term = cw[tap:tap + 1, :] * up[pl.ds(s0 + (HALO + tap - CONV_WIDTH // 2), CONV_PIECE), :]
                y = term if y is None else y + term
            y = y / (1.0 + jnp.exp(-y))
            if scale is not None:
                y = y * lax.rsqrt(jnp.sum(y * y, axis=-1, keepdims=True) + EPS)
                if scale != 1.0:
                    y = y * scale
            dst_ref[pl.ds(s0, CONV_PIECE), :] = y

    def conv_rows(rows):
        for r0 in rows:
            for _, cw_ref, up, dst_ref, scale in streams:
                conv_block(r0, cw_ref, up, dst_ref, scale)
                yield

    edge = unroll * c
    all_rows = range(0, seq, CONV_ROWS)
    edge_rows = [r for r in all_rows if r < edge or r + CONV_ROWS > seq - edge]
    mid_rows = [r for r in all_rows if r not in edge_rows]
    for _ in conv_rows(edge_rows):
        pass

    ii = lax.broadcasted_iota(jnp.int32, (c, PAIR), 0)
    ll = lax.broadcasted_iota(jnp.int32, (c, PAIR), 1)
    jj = ll & (c - 1)
    is_f = ll < c
    ahead = jnp.where(is_f, ii - jj, jj - ii)
    incl = ahead >= 0
    strict = ahead > 0
    eye = ii == jj
    xor = ii ^ jj

    rr = lax.broadcasted_iota(jnp.int32, (2 * PAIR, 2 * PAIR), 0)
    nn = lax.broadcasted_iota(jnp.int32, (2 * PAIR, 2 * PAIR), 1)
    bd1[...] = jnp.where(((rr & (PAIR - 1)) >> (c.bit_length() - 1)) == (nn >> (PAIR.bit_length() - 1)),
                         1.0, 0.0).astype(BF16)

    def blockdiag(xp):
        top = jnp.where(is_f, xp, 0.0)
        bot = jnp.where(is_f, 0.0, xp)
        return jnp.concatenate([top, bot], axis=0).astype(BF16)

    is_b = ll >= c
    is_f2 = lax.broadcasted_iota(jnp.int32, (2 * c, PAIR), 1) < c
    zer = jnp.zeros((c, LANES), BF16)

    zer2 = jnp.zeros((PAIR, PAIR), BF16)

    def blockdiag_many(xps):
        if len(xps) == 1:
            return blockdiag(xps[0])
        b0, b1 = blockdiag(xps[0]), blockdiag(xps[1])
        return jnp.concatenate([jnp.concatenate([b0, zer2], axis=1), jnp.concatenate([zer2, b1], axis=1)], axis=0)

    def side_by_side(xs):
        return jnp.concatenate([x.astype(BF16) for x in xs], axis=1) if len(xs) > 1 else xs[0].astype(BF16)

    def prep_group(pairs):
        st = []
        for i, slot in pairs:
            ra = _aligned(i * c, c)
            rb = _aligned((nc - 1 - i) * c, c)
            s = dict(slot=slot, ka=ks[pl.ds(ra, c), :], qa=qs[pl.ds(ra, c), :],
                     kz=ks[pl.ds(rb, c), :], qz=qs[pl.ds(rb, c), :])
            ka16, qa16, kz16, qz16 = (s[n].astype(BF16) for n in ("ka", "qa", "kz", "qz"))
            va16 = vs[pl.ds(ra, c), :].astype(BF16)
            vz16 = vs[pl.ds(rb, c), :].astype(BF16)
            s["bdv"] = jnp.concatenate([jnp.concatenate([va16, zer], axis=1),
                                        jnp.concatenate([zer, vz16], axis=1)], axis=0)
            s["bdk"] = jnp.concatenate([jnp.concatenate([ka16, zer], axis=1),
                                        jnp.concatenate([zer, kz16], axis=1)], axis=0)
            s["gram"] = lax.dot_general(
                jnp.concatenate([jnp.concatenate([ka16, qa16], axis=0), jnp.concatenate([kz16, qz16], axis=0)], axis=1),
                s["bdk"], NT_DIMS, preferred_element_type=F32)
            tile = gp_ref[i]
            s["brow"] = tile[0:1, :]
            grow = tile[1:2, :]
            x = jnp.concatenate([jnp.where(incl, grow, 0.0), jnp.where(eye, s["brow"], 0.0)], axis=0)
            hi, lo = _split2(x)
            s["cb"] = _dot(jnp.concatenate([hi, lo], axis=1), bd1[...])
            st.append(s)
        yield
        for s in st:
            cb, gram = s["cb"], s["gram"]
            s["cf_b"] = cb[0:c, 0:PAIR]
            s["cb_b"] = cb[0:c, PAIR:2 * PAIR]
            cpair = jnp.where(is_f, s["cf_b"], s["cb_b"])
            bpair = jnp.where(is_f, cb[c:2 * c, 0:PAIR], cb[c:2 * c, PAIR:2 * PAIR])
            s["crow"] = jnp.sum(jnp.where(eye, cpair, 0.0), axis=0, keepdims=True)
            decay = jnp.where(incl, jnp.exp(jnp.where(incl, cpair - s["crow"], 0.0)), 0.0)
            s["a"] = jnp.where(strict, bpair * gram[0:c, :] * decay, 0.0)
            s["attn16"] = (gram[c:2 * c, :] * decay).astype(BF16)

        ts = [jnp.where(eye, 1.0, 0.0) - jnp.where(xor == 1, s["a"], 0.0) for s in st]
        for lvl in range(1, 6):
            es = [jnp.where((xor >> lvl) == 1, s["a"], 0.0) for s in st]
            te = _dot(side_by_side(ts), blockdiag_many(es))
            yield
            tet = _dot(te.astype(BF16), blockdiag_many(ts))
            yield
            ts = [t - tet[:, n * PAIR:(n + 1) * PAIR] for n, t in enumerate(ts)]

        for s, t in zip(st, ts):
            erow = jnp.exp(s["crow"])
            s["u"] = _dot((t * s["brow"]).astype(BF16), s["bdv"])
            s["w"] = _dot((t * (s["brow"] * erow)).astype(BF16), s["bdk"])
        yield

        for s in st:
            s["last_f"] = s["cf_b"][c - 1:c, :]
            s["last_b"] = s["cb_b"][0:1, :]
            s["dirs"] = ((0, s["ka"], s["qa"], s["cf_b"], s["last_f"], is_f, pq_f, n_f, oc_f),
                         (LANES, s["kz"], s["qz"], s["cb_b"], s["last_b"], is_b, pq_b, n_b, oc_b))
            s["prods"] = []
            for (lo, kx, qx, cx, last, keep, pq, nn_ref, oc) in s["dirs"]:
                uw = jnp.concatenate([s["u"][:, lo:lo + LANES], s["w"][:, lo:lo + LANES]], axis=1).astype(BF16)
                kdec = (kx * jnp.exp(last - cx)).astype(BF16)
                n_p = lax.dot_general(kdec, uw, TN_DIMS, preferred_element_type=F32)
                o_aw = _dot(jnp.where(keep, s["attn16"], zer), jnp.concatenate([uw, uw], axis=0))
                s["prods"].append((n_p, o_aw))
        yield
        for s in st:
            slot = s["slot"]
            p0 = _aligned(slot * (3 * c), c)
            n0 = _aligned(slot * (2 * c), 2 * c)
            o0 = _aligned(slot * c, c)
            for (lo, kx, qx, cx, last, keep, pq, nn_ref, oc), (n_p, o_aw) in zip(s["dirs"], s["prods"]):
                pq[pl.ds(p0, 2 * c), :] = n_p[:, LANES:2 * LANES].astype(BF16)
                pq[pl.ds(p0 + 2 * c, c), :] = (qx * jnp.exp(cx) - o_aw[:, LANES:2 * LANES]).astype(BF16)
                nn_ref[pl.ds(n0, 2 * c), :] = n_p[:, 0:LANES]
                oc[pl.ds(o0, c), :] = o_aw[:, 0:LANES]
            el_scr[slot] = jnp.concatenate([jnp.exp(s["last_f"]), jnp.exp(s["last_b"]),
                                            jnp.zeros((SUBLANES - 2, LANES), F32)], axis=0)

    def rec_block(blk, state):
        s_f, s_b = state
        for j in range(unroll):
            i = blk * unroll + j
            slot = (blk & 1) * unroll + j
            p0 = _aligned(slot * (3 * c), c)
            n0 = _aligned(slot * (2 * c), 2 * c)
            o0 = _aligned(slot * c, c)
            el = el_scr[slot]
            r_f = _dot(pq_f[pl.ds(p0, 3 * c), :], s_f.astype(BF16))
            r_b = _dot(pq_b[pl.ds(p0, 3 * c), :], s_b.astype(BF16))
            yield
            of[pl.ds(_aligned(i * c, c), c), :] = oc_f[pl.ds(o0, c), :] + r_f[2 * c:3 * c, :]
            s_f = s_f * el[0:1, :] + n_f[pl.ds(n0, 2 * c), :] - r_f[0:2 * c, :]
            ob[pl.ds(_aligned((nc - 1 - i) * c, c), c), :] = oc_b[pl.ds(o0, c), :] + r_b[2 * c:3 * c, :]
            s_b = s_b * el[1:2, :] + n_b[pl.ds(n0, 2 * c), :] - r_b[0:2 * c, :]
        state[0], state[1] = s_f, s_b

    def run_interleaved(gens, rec=None):
        rec_period = max(1, (PREP_STAGES * len(gens)) // (unroll + 1))
        steps = 0
        while gens or rec is not None:
            alive = []
            for g in gens:
                if _advance(g):
                    alive.append(g)
                steps += 1
                if rec is not None and steps % rec_period == 0 and not _advance(rec):
                    rec = None
            gens = alive
            if not gens and rec is not None and not _advance(rec):
                rec = None

    def prep_gens(blk):
        pairs = [(blk * unroll + j, (blk & 1) * unroll + j) for j in range(unroll)]
        return [prep_group(pairs[j:j + DN_GROUP]) for j in range(0, unroll, DN_GROUP)]

    def pipelined(blk, carry):
        state = list(carry)
        run_interleaved(prep_gens(blk), rec_block(blk - 1, state))
        return tuple(state)

    def finish_rows(rows):
        hn = hn_ref[...]
        for r0 in rows:
            for s0 in range(r0, r0 + CONV_ROWS, CONV_PIECE):
                o = of[pl.ds(s0, CONV_PIECE), :] + ob[pl.ds(s0, CONV_PIECE), :]
                zz = z_ref[pl.ds(s0, CONV_PIECE), :].astype(F32)
                o_ref[pl.ds(s0, CONV_PIECE), :] = (_rms(o, hn) * (zz / (1.0 + jnp.exp(-zz)))).astype(BF16)
            yield

    run_interleaved(prep_gens(0) + [conv_rows(mid_rows)])
    s0 = jnp.zeros((LANES, LANES), F32)
    state = list(lax.fori_loop(1, nblk, pipelined, (s0, s0)))
    run_interleaved([finish_rows(mid_rows)] if nblk > 1 else [], rec_block(nblk - 1, state))
    for _ in finish_rows(edge_rows if nblk > 1 else all_rows):
        pass


DN_UNROLL = 16
DN_GROUP = 1
DN_MIN_BLOCKS = 4


def _deltanet(p, cw, gp, hn, *, batch, seq):
    assert seq % CONV_ROWS == 0 and CONV_ROWS % CHUNK == 0, (seq, CONV_ROWS)
    nc = seq // CHUNK
    unroll = math.gcd(nc, DN_UNROLL)
    while unroll > 1 and nc // unroll < DN_MIN_BLOCKS:
        unroll //= 2
    h = DN_HEADS
    slots = 2 * unroll
    c = CHUNK
    col = lambda off: pl.BlockSpec((None, seq, LANES), lambda b, hd, off=off: (b, 0, off + hd))
    cws = lambda off: pl.BlockSpec((SUBLANES, LANES), lambda b, hd, off=off: (0, off + hd))
    return pl.pallas_call(
        functools.partial(_deltanet_body, seq=seq, unroll=unroll),
        grid=(batch, h),
        in_specs=[col(0), col(h), col(2 * h), col(3 * h), cws(0), cws(h), cws(2 * h),
                  pl.BlockSpec((nc, SUBLANES, LANES), lambda b, hd: (b, hd, 0)),
                  pl.BlockSpec((1, LANES), lambda b, hd: (0, 0))],
        out_specs=pl.BlockSpec((None, seq, LANES), lambda b, hd: (b, 0, hd)),
        out_shape=jax.ShapeDtypeStruct((batch, seq, h * LANES), BF16),
        scratch_shapes=[
            pltpu.VMEM((seq + 2 * HALO, LANES), F32),
            pltpu.VMEM((seq + 2 * HALO, LANES), F32),
            pltpu.VMEM((seq + 2 * HALO, LANES), F32),
            pltpu.VMEM((seq, LANES), F32),
            pltpu.VMEM((seq, LANES), F32),
            pltpu.VMEM((seq, LANES), F32),
            pltpu.VMEM((slots * 3 * c, LANES), BF16),
            pltpu.VMEM((slots * 3 * c, LANES), BF16),
            pltpu.VMEM((slots * 2 * c, LANES), F32),
            pltpu.VMEM((slots * 2 * c, LANES), F32),
            pltpu.VMEM((slots * c, LANES), F32),
            pltpu.VMEM((slots * c, LANES), F32),
            pltpu.VMEM((slots, SUBLANES, LANES), F32),
            pltpu.VMEM((seq, LANES), F32),
            pltpu.VMEM((seq, LANES), F32),
            pltpu.VMEM((2 * PAIR, 2 * PAIR), BF16),
        ],
        compiler_params=_params("parallel", "parallel"),
        name="deltanet",
    )(p, p, p, p, cw, cw, cw, gp, hn)


TW_ROWS = 64


def _twiddle_body(ac_ref, as_ref, bc_ref, bs_ref, c_ref, s_ref):
    ac = ac_ref[...]
    sn = as_ref[...]
    bc = bc_ref[...]
    bs = bs_ref[...]
    c_ref[...] = (ac * bc - sn * bs).astype(BF16)
    s_ref[...] = (-(sn * bc + ac * bs)).astype(BF16)


def _twiddle(seq):
    nblk = seq // 2 // TW_ROWS
    kk = jnp.arange(seq, dtype=jnp.int32)[None, :]
    step = 2.0 * math.pi / seq
    ang_a = ((jnp.arange(TW_ROWS, dtype=jnp.int32)[:, None] * kk) % seq).astype(F32) * step
    ang_b = ((jnp.arange(nblk, dtype=jnp.int32)[:, None] * TW_ROWS * kk) % seq).astype(F32) * step
    tab = pl.BlockSpec((TW_ROWS, seq), lambda i: (0, 0))
    row = pl.BlockSpec((None, 1, seq), lambda i: (i, 0, 0))
    out = pl.BlockSpec((TW_ROWS, seq), lambda i: (i, 0))
    return pl.pallas_call(
        _twiddle_body,
        grid=(nblk,),
        in_specs=[tab, tab, row, row],
        out_specs=[out, out],
        out_shape=[jax.ShapeDtypeStruct((seq // 2, seq), BF16)] * 2,
        compiler_params=_params("parallel"),
        name="twiddle",
    )(jnp.cos(ang_a), jnp.sin(ang_a), jnp.cos(ang_b)[:, None, :], jnp.sin(ang_b)[:, None, :])


def _chandft_body(u_ref, w_ref, rc_ref, rs_ref):
    gd = w_ref.shape[0]
    w = w_ref[...]
    for g in range(FN_GROUPS):
        y = _dot(u_ref[:, g * gd:(g + 1) * gd], w)
        rc_ref[:, g * gd:(g + 1) * gd] = y[:, 0:gd].astype(BF16)
        rs_ref[:, g * gd:(g + 1) * gd] = y[:, gd:2 * gd].astype(BF16)


def _chandft(p, wch, *, col_block, tm=512):
    t = p.shape[0]
    dfn = FN_GROUPS * wch.shape[0]
    out = pl.BlockSpec((tm, dfn), lambda i: (i, 0))
    return pl.pallas_call(
        _chandft_body,
        grid=(t // tm,),
        in_specs=[pl.BlockSpec((tm, dfn), lambda i: (i, col_block)),
                  pl.BlockSpec(wch.shape, lambda i: (0, 0))],
        out_specs=[out, out],
        out_shape=[jax.ShapeDtypeStruct((t, dfn), BF16)] * 2,
        compiler_params=_params("parallel"),
        name="chandft",
    )(p, wch)


def _seqdft_body(c_ref, s_ref, alt_ref, rc_ref, rs_ref, lo_ref, hi_ref, mid_ref):
    p = _dot(c_ref[...], rc_ref[...])
    mq = _dot(s_ref[...], rs_ref[...])
    lo_ref[...] = (p + mq).astype(BF16)
    hi_ref[...] = (p - mq).astype(BF16)

    @pl.when(pl.program_id(1) == 0)
    def _():
        mid_ref[...] = _dot(alt_ref[...], rc_ref[...]).astype(BF16)


def _seqdft(ctw, stw, rc, rs, *, batch, seq, tm=256):
    dfn = rc.shape[-1]
    half = seq // 2
    sign = 1.0 - 2.0 * (jnp.arange(seq, dtype=jnp.int32) % 2).astype(F32)
    alt = jnp.pad(sign[None, :], ((0, SUBLANES - 1), (0, 0))).astype(BF16)
    tw = pl.BlockSpec((tm, seq), lambda b, i: (i, 0))
    rhs = pl.BlockSpec((None, seq, dfn), lambda b, i: (b, 0, 0))
    out = pl.BlockSpec((None, tm, dfn), lambda b, i: (b, i, 0))
    lo, hi, mid = pl.pallas_call(
        _seqdft_body,
        grid=(batch, half // tm),
        in_specs=[tw, tw, pl.BlockSpec((SUBLANES, seq), lambda b, i: (0, 0)), rhs, rhs],
        out_specs=[out, out, pl.BlockSpec((None, SUBLANES, dfn), lambda b, i: (b, 0, 0))],
        out_shape=[jax.ShapeDtypeStruct((batch, half, dfn), BF16)] * 2
        + [jax.ShapeDtypeStruct((batch, SUBLANES, dfn), BF16)],
        compiler_params=_params("parallel", "arbitrary"),
        name="seqdft",
    )(ctw, stw, alt, rc.reshape(batch, seq, dfn), rs.reshape(batch, seq, dfn))
    return jnp.concatenate([lo, mid[:, 0:1], jnp.flip(hi[:, 1:], axis=1)], axis=1)


def _outproj_body(x_ref, a_ref, b_ref, wa_ref, wb_ref, o_ref):
    o_ref[...] = x_ref[...] + _dot(a_ref[...], wa_ref[...]) + _dot(b_ref[...], wb_ref[...])


def _outproj(x, a, b, wa, wb, *, tm=512):
    t, d = x.shape
    da = a.shape[1]
    db = b.shape[1]
    return pl.pallas_call(
        _outproj_body,
        grid=(t // tm,),
        in_specs=[pl.BlockSpec((tm, d), lambda i: (i, 0)),
                  pl.BlockSpec((tm, da), lambda i: (i, 0)),
                  pl.BlockSpec((tm, db), lambda i: (i, 0)),
                  pl.BlockSpec((da, d), lambda i: (0, 0)),
                  pl.BlockSpec((db, d), lambda i: (0, 0))],
        out_specs=pl.BlockSpec((tm, d), lambda i: (i, 0)),
        out_shape=jax.ShapeDtypeStruct((t, d), F32),
        compiler_params=_params("parallel"),
        name="outproj",
    )(x, a, b, wa, wb)


def _kvproj_body(m_ref, g_ref, w_ref, o_ref, h_ref):
    @pl.when(pl.program_id(1) == 0)
    def _():
        h_ref[...] = _rms(m_ref[...], g_ref[...]).astype(BF16)

    o_ref[...] = _dot(h_ref[...], w_ref[...]).astype(BF16)


def _kvproj(mem, g, w, *, tn=1024):
    b, n, d = mem.shape
    nout = w.shape[1]
    return pl.pallas_call(
        _kvproj_body,
        grid=(b, nout // tn),
        in_specs=[pl.BlockSpec((None, n, d), lambda i, j: (i, 0, 0)),
                  pl.BlockSpec((1, d), lambda i, j: (0, 0)),
                  pl.BlockSpec((d, tn), lambda i, j: (0, j))],
        out_specs=pl.BlockSpec((None, n, tn), lambda i, j: (i, 0, j)),
        out_shape=jax.ShapeDtypeStruct((b, n, nout), BF16),
        scratch_shapes=[pltpu.VMEM((n, d), BF16)],
        compiler_params=_params("parallel", "arbitrary"),
        name="kvproj",
    )(mem, g, w)


def _xattn_body(x_ref, g_ref, wq_ref, kv_ref, wo_ref, o_ref, q_scr, a_scr):
    d = x_ref.shape[-1]
    hd = d // XA_HEADS
    x = x_ref[...]
    q_scr[...] = _dot(_rms(x, g_ref[...]).astype(BF16), wq_ref[...]).astype(BF16)
    for h in range(XA_HEADS):
        kh = kv_ref[:, h * hd:(h + 1) * hd]
        vh = kv_ref[:, d + h * hd:d + (h + 1) * hd]
        s = lax.dot_general(q_scr[:, h * hd:(h + 1) * hd], kh, NT_DIMS,
                            preferred_element_type=F32) * (hd ** -0.5)
        p = jnp.exp(s - jnp.max(s, axis=-1, keepdims=True))
        l = jnp.sum(p, axis=-1, keepdims=True)
        a_scr[:, h * hd:(h + 1) * hd] = (_dot(p.astype(BF16), vh) / l).astype(BF16)
    o_ref[...] = x + _dot(a_scr[...], wo_ref[...])


def _xattn(x, g, wq, kv, wo, *, batch, seq, tm=256):
    d = x.shape[-1]
    n = kv.shape[1]
    return pl.pallas_call(
        _xattn_body,
        grid=(batch, seq // tm),
        in_specs=[pl.BlockSpec((None, tm, d), lambda b, i: (b, i, 0)),
                  pl.BlockSpec((1, d), lambda b, i: (0, 0)),
                  pl.BlockSpec((d, d), lambda b, i: (0, 0)),
                  pl.BlockSpec((None, n, 2 * d), lambda b, i: (b, 0, 0)),
                  pl.BlockSpec((d, d), lambda b, i: (0, 0))],
        out_specs=pl.BlockSpec((None, tm, d), lambda b, i: (b, i, 0)),
        out_shape=jax.ShapeDtypeStruct((batch, seq, d), F32),
        scratch_shapes=[pltpu.VMEM((tm, d), BF16), pltpu.VMEM((tm, d), BF16)],
        compiler_params=_params("parallel", "arbitrary"),
        name="xattn",
    )(x.reshape(batch, seq, d), g, wq, kv, wo)


def _prep_weights(ffn1_norm, ffn1_w_gate, ffn1_w_up, ffn1_w_down, mix_norm, w_in, conv_w, a_log, dt_bias,
                  dn_head_norm, w_out, xattn_norm, mem_norm, xattn_w_q, xattn_w_kv, xattn_w_o, ffn2_norm,
                  ffn2_w_gate, ffn2_w_up, ffn2_w_down, final_norm, *, tm_inproj):
    d = w_in.shape[0]
    dd = DN_HEADS * DN_HEAD_DIM

    def ffn_w(wg, wu, wd):
        return (wg.astype(BF16), wu.astype(BF16), wd.astype(BF16))

    row = lambda v: v.reshape(1, -1).astype(F32)
    off = 4 * dd
    w_main = jnp.concatenate([w_in[:, :off], w_in[:, off + 4 * DN_HEADS:]], axis=1).astype(BF16)
    beta_w = w_in[:, off:off + 2 * DN_HEADS].reshape(d, 2, DN_HEADS)
    a_w = w_in[:, off + 2 * DN_HEADS:off + 4 * DN_HEADS].reshape(d, 2, DN_HEADS)
    gate_w = jnp.concatenate([beta_w, a_w, jnp.zeros((d, SUBLANES - 4, DN_HEADS), F32)], axis=1)
    wgt = gate_w.transpose(2, 1, 0).reshape(DN_HEADS * SUBLANES, d).astype(BF16)

    def gate_param(v):
        full = jnp.concatenate([jnp.zeros((2, DN_HEADS), F32), v.astype(F32),
                                jnp.zeros((SUBLANES - 4, DN_HEADS), F32)], axis=0)
        return jnp.broadcast_to(full.T.reshape(DN_HEADS * SUBLANES, 1), (DN_HEADS * SUBLANES, tm_inproj))

    gd = (w_in.shape[1] - off - 4 * DN_HEADS) // FN_GROUPS
    idx = jnp.arange(gd, dtype=jnp.int32)
    ang = ((idx[:, None] * idx[None, :]) % gd).astype(F32) * (2.0 * math.pi / gd)
    return dict(
        ffn1=(row(ffn1_norm),) + ffn_w(ffn1_w_gate, ffn1_w_up, ffn1_w_down),
        ffn2=(row(ffn2_norm),) + ffn_w(ffn2_w_gate, ffn2_w_up, ffn2_w_down),
        final=row(final_norm), mix=row(mix_norm), w_main=w_main, wgt=wgt,
        alog=gate_param(a_log), dt=gate_param(dt_bias),
        conv=jnp.pad(conv_w.astype(F32), ((0, SUBLANES - CONV_WIDTH), (0, 0))),
        hn=row(dn_head_norm), chan=(jnp.cos(ang), jnp.sin(ang)),
        wout_a=w_out[:dd].astype(BF16), wout_b=w_out[dd:].astype(BF16),
        xn=row(xattn_norm), mn=row(mem_norm), wq=xattn_w_q.astype(BF16), wkv=xattn_w_kv.astype(BF16),
        wo=xattn_w_o.astype(BF16))


def _gate_pairs(gt, batch, seq):
    nc = seq // CHUNK
    g = gt.reshape(DN_HEADS, SUBLANES, batch, nc, CHUNK)
    rev = lambda a: a[:, :, ::-1, :]
    pairs = jnp.stack([jnp.concatenate([g[:, 0], rev(g[:, 1])], axis=-1),
                       jnp.concatenate([g[:, 2], rev(g[:, 3])], axis=-1)], axis=1)
    pairs = jnp.pad(pairs, ((0, 0), (0, SUBLANES - 2), (0, 0), (0, 0), (0, 0)))
    return pairs.transpose(2, 3, 0, 1, 4).reshape(batch * nc, DN_HEADS * SUBLANES, 2 * CHUNK)


def _trunk(x, mem, w, tm_inproj, final):
    batch, seq, d = x.shape
    t = batch * seq
    dd = DN_HEADS * DN_HEAD_DIM
    x0 = x.reshape(t, d)
    x1 = _ffn(x0, *w["ffn1"], w["final"], final=False)
    p, gt = _inproj(x1, w["mix"], w["w_main"], w["wgt"], w["alog"], w["dt"], tm=tm_inproj)
    gp = _gate_pairs(gt, batch, seq)
    o_dn = _deltanet(p.reshape(batch, seq, -1), w["conv"], gp, w["hn"], batch=batch, seq=seq)
    dfn = p.shape[1] - 4 * dd
    scale = (seq * (dfn // FN_GROUPS)) ** -0.5
    wch = (jnp.concatenate(w["chan"], axis=1) * scale).astype(BF16)
    rc, rs = _chandft(p, wch, col_block=4 * dd // dfn)
    ctw, stw = _twiddle(seq)
    o_fn = _seqdft(ctw, stw, rc, rs, batch=batch, seq=seq)
    x2 = _outproj(x1, o_dn.reshape(t, dd), o_fn.reshape(t, dfn), w["wout_a"], w["wout_b"])
    kv = _kvproj(mem, w["mn"], w["wkv"])
    x3 = _xattn(x2, w["xn"], w["wq"], kv, w["wo"], batch=batch, seq=seq)
    y = _ffn(x3.reshape(t, d), *w["ffn2"], w["final"], final=final)
    return y.reshape(batch, seq, d)


def kernel(x_prompt, x_sample, mem_prompt, mem_sample, ffn1_norm, ffn1_w_gate, ffn1_w_up, ffn1_w_down, mix_norm,
           w_in, conv_w, a_log, dt_bias, dn_head_norm, w_out, xattn_norm, mem_norm, xattn_w_q, xattn_w_kv,
           xattn_w_o, ffn2_norm, ffn2_w_gate, ffn2_w_up, ffn2_w_down, final_norm):
    depth = ffn1_norm.shape[0]
    tm_inproj = 512
    y_prompt, y_sample = x_prompt, x_sample
    for l in range(depth):
        w = _prep_weights(ffn1_norm[l], ffn1_w_gate[l], ffn1_w_up[l], ffn1_w_down[l], mix_norm[l], w_in[l],
                          conv_w[l], a_log[l], dt_bias[l], dn_head_norm[l], w_out[l], xattn_norm[l], mem_norm[l],
                          xattn_w_q[l], xattn_w_kv[l], xattn_w_o[l], ffn2_norm[l], ffn2_w_gate[l], ffn2_w_up[l],
                          ffn2_w_down[l], final_norm, tm_inproj=tm_inproj)
        y_prompt = _trunk(y_prompt, mem_prompt, w, tm_inproj, l == depth - 1)
        y_sample = _trunk(y_sample, mem_sample, w, tm_inproj, l == depth - 1)
    return (y_prompt, y_sample)
```

```python
import functools
import math

import jax
import jax.numpy as jnp
from jax import lax
from jax.experimental import pallas as pl
from jax.experimental.pallas import tpu as pltpu

F32 = jnp.float32
BF16 = jnp.bfloat16
EPS = 1e-6

DN_HEADS = 8
DN_HEAD_DIM = 128
FN_GROUPS = 4
CONV_WIDTH = 5
CHUNK = 64
XA_HEADS = 4
N_MEM = 256

LANES = 128
SUBLANES = 8
VMEM_LIMIT = 56 * 1024 * 1024

NT_DIMS = (((1,), (1,)), ((), ()))
TN_DIMS = (((0,), (0,)), ((), ()))


def _params(*sem):
    return pltpu.CompilerParams(dimension_semantics=sem, vmem_limit_bytes=VMEM_LIMIT)


def _dot(a, b):
    return jnp.dot(a, b, preferred_element_type=F32)


def _rms(x, g):
    ms = jnp.mean(x * x, axis=-1, keepdims=True)
    return x * lax.rsqrt(ms + EPS) * g


def _sigmoid(x):
    return 1.0 / (1.0 + jnp.exp(-x))


def _ffn_body(x_ref, g_ref, wg_ref, wu_ref, wd_ref, fin_ref, o_ref, h_ref, *, final, last_width):
    j = pl.program_id(1)
    nj = pl.num_programs(1)
    tf = wg_ref.shape[1]

    @pl.when(j == 0)
    def _():
        h_ref[...] = _rms(x_ref[...], g_ref[...]).astype(BF16)
        o_ref[...] = jnp.zeros_like(o_ref)

    def hidden_block(width):
        h = h_ref[...]
        a = _dot(h, wg_ref[:, 0:width])
        u = _dot(h, wu_ref[:, 0:width])
        act = (a * _sigmoid(a) * u).astype(BF16)
        o_ref[...] += _dot(act, wd_ref[0:width, :])

    if last_width == tf:
        hidden_block(tf)
    else:
        pl.when(j < nj - 1)(lambda: hidden_block(tf))
        pl.when(j == nj - 1)(lambda: hidden_block(last_width))

    @pl.when(j == nj - 1)
    def _():
        y = x_ref[...] + 0.5 * o_ref[...]
        if final:
            y = _rms(y, fin_ref[...])
        o_ref[...] = y


def _ffn(x, g, wg, wu, wd, fin, *, final, tm=1024, tf=512):
    t, d = x.shape
    tm = min(tm, t)
    assert t % tm == 0, (t, tm)
    dff = wg.shape[1]
    nj = pl.cdiv(dff, tf)
    last_width = dff - (nj - 1) * tf
    assert last_width % LANES == 0, (dff, tf)
    return pl.pallas_call(
        functools.partial(_ffn_body, final=final, last_width=last_width),
        grid=(t // tm, nj),
        in_specs=[
            pl.BlockSpec((tm, d), lambda i, j: (i, 0)),
            pl.BlockSpec((1, d), lambda i, j: (0, 0)),
            pl.BlockSpec((d, tf), lambda i, j: (0, j)),
            pl.BlockSpec((d, tf), lambda i, j: (0, j)),
            pl.BlockSpec((tf, d), lambda i, j: (j, 0)),
            pl.BlockSpec((1, d), lambda i, j: (0, 0)),
        ],
        out_specs=pl.BlockSpec((tm, d), lambda i, j: (i, 0)),
        out_shape=jax.ShapeDtypeStruct((t, d), F32),
        scratch_shapes=[pltpu.VMEM((tm, d), BF16)],
        compiler_params=_params("parallel", "arbitrary"),
        name="ffn_final" if final else "ffn",
    )(x, g, wg, wu, wd, fin)


def _inproj_body(x_ref, g_ref, w_ref, wgt_ref, alog_ref, dt_ref, p_ref, gt_ref, h_ref):
    j = pl.program_id(1)

    @pl.when(j == 0)
    def _():
        h = _rms(x_ref[...], g_ref[...]).astype(BF16)
        h_ref[...] = h
        gt = lax.dot_general(wgt_ref[...], h, NT_DIMS, preferred_element_type=F32)
        kind = lax.broadcasted_iota(jnp.int32, gt.shape, 0) & (SUBLANES - 1)
        xs = gt + dt_ref[...]
        softplus = jnp.maximum(xs, 0.0) + jnp.log1p(jnp.exp(-jnp.abs(xs)))
        log_decay = -jnp.exp(alog_ref[...]) * softplus
        gt_ref[...] = jnp.where(kind < 2, _sigmoid(gt), jnp.where(kind < 4, log_decay, 0.0))

    p_ref[...] = _dot(h_ref[...], w_ref[...]).astype(BF16)


def _inproj(x, g, w, wgt, alog, dt, *, tm=512, tn=2560):
    t, d = x.shape
    n = w.shape[1]
    rows = wgt.shape[0]
    return pl.pallas_call(
        _inproj_body,
        grid=(t // tm, n // tn),
        in_specs=[
            pl.BlockSpec((tm, d), lambda i, j: (i, 0)),
            pl.BlockSpec((1, d), lambda i, j: (0, 0)),
            pl.BlockSpec((d, tn), lambda i, j: (0, j)),
            pl.BlockSpec((rows, d), lambda i, j: (0, 0)),
            pl.BlockSpec((rows, tm), lambda i, j: (0, 0)),
            pl.BlockSpec((rows, tm), lambda i, j: (0, 0)),
        ],
        out_specs=[
            pl.BlockSpec((tm, tn), lambda i, j: (i, j)),
            pl.BlockSpec((rows, tm), lambda i, j: (0, i)),
        ],
        out_shape=[jax.ShapeDtypeStruct((t, n), BF16), jax.ShapeDtypeStruct((rows, t), F32)],
        scratch_shapes=[pltpu.VMEM((tm, d), BF16)],
        compiler_params=_params("parallel", "arbitrary"),
        name="inproj",
    )(x, g, w, wgt, alog, dt)


CONV_ROWS = 512
CONV_PIECE = 128
HALO = SUBLANES
PAIR = 2 * CHUNK


def _split2(x):
    hi = x.astype(BF16)
    lo = (x - hi.astype(F32)).astype(BF16)
    return hi, lo


def _aligned(x, m):
    return x if isinstance(x, int) else pl.multiple_of(x, m)


def _advance(gen):
    try:
        next(gen)
        return True
    except StopIteration:
        return False


PREP_STAGES = 14


def _deltanet_body(q_ref, k_ref, v_ref, z_ref, cwq_ref, cwk_ref, cwv_ref, gp_ref, hn_ref, o_ref,
                   upq, upk, upv, qs, ks, vs, pq_f, pq_b, n_f, n_b, oc_f, oc_b, el_scr, of, ob, bd1,
                   *, seq, unroll):
    nc = seq // CHUNK
    c = CHUNK
    nblk = nc // unroll

    streams = ((q_ref, cwq_ref, upq, qs, DN_HEAD_DIM ** -0.5), (k_ref, cwk_ref, upk, ks, 1.0),
               (v_ref, cwv_ref, upv, vs, None))
    for src_ref, _, up, _, _ in streams:
        zeros = jnp.zeros((HALO, LANES), F32)
        up[0:HALO, :] = zeros
        up[seq + HALO:seq + 2 * HALO, :] = zeros

        def fill(r, _, src_ref=src_ref, up=up):
            r0 = pl.multiple_of(r * CONV_ROWS, CONV_ROWS)
            up[pl.ds(r0 + HALO, CONV_ROWS), :] = src_ref[pl.ds(r0, CONV_ROWS), :].astype(F32)
            return 0

        lax.fori_loop(0, seq // CONV_ROWS, fill, 0)

    def conv_block(r0, cw_ref, up, dst_ref, scale):
        cw = cw_ref[...]
        for s0 in range(r0, r0 + CONV_ROWS, CONV_PIECE):
            y = None
            for tap in range(CONV_WIDTH):
                term = cw[tap:tap + 1, :] * up[pl.ds(s0 + (HALO + tap - CONV_WIDTH // 2), CONV_PIECE), :]
                y = term if y is None else y + term
            y = y / (1.0 + jnp.exp(-y))
            if scale is not None:
                y = y * lax.rsqrt(jnp.sum(y * y, axis=-1, keepdims=True) + EPS)
                if scale != 1.0:
                    y = y * scale
            dst_ref[pl.ds(s0, CONV_PIECE), :] = y

    def conv_rows(rows):
        for r0 in rows:
            for _, cw_ref, up, dst_ref, scale in streams:
                conv_block(r0, cw_ref, up, dst_ref, scale)
                yield

    edge = unroll * c
    all_rows = range(0, seq, CONV_ROWS)
    edge_rows = [r for r in all_rows if r < edge or r + CONV_ROWS > seq - edge]
    mid_rows = [r for r in all_rows if r not in edge_rows]
    for _ in conv_rows(edge_rows):
        pass

    ii = lax.broadcasted_iota(jnp.int32, (c, PAIR), 0)
    ll = lax.broadcasted_iota(jnp.int32, (c, PAIR), 1)
    jj = ll & (c - 1)
    is_f = ll < c
    ahead = jnp.where(is_f, ii - jj, jj - ii)
    incl = ahead >= 0
    strict = ahead > 0
    eye = ii == jj
    xor = ii ^ jj

    rr = lax.broadcasted_iota(jnp.int32, (2 * PAIR, 2 * PAIR), 0)
    nn = lax.broadcasted_iota(jnp.int32, (2 * PAIR, 2 * PAIR), 1)
    bd1[...] = jnp.where(((rr & (PAIR - 1)) >> (c.bit_length() - 1)) == (nn >> (PAIR.bit_length() - 1)),
                         1.0, 0.0).astype(BF16)

    def blockdiag(xp):
        top = jnp.where(is_f, xp, 0.0)
        bot = jnp.where(is_f, 0.0, xp)
        return jnp.concatenate([top, bot], axis=0).astype(BF16)

    is_b = ll >= c
    is_f2 = lax.broadcasted_iota(jnp.int32, (2 * c, PAIR), 1) < c
    zer = jnp.zeros((c, LANES), BF16)

    zer2 = jnp.zeros((PAIR, PAIR), BF16)

    def blockdiag_many(xps):
        if len(xps) == 1:
            return blockdiag(xps[0])
        b0, b1 = blockdiag(xps[0]), blockdiag(xps[1])
        return jnp.concatenate([jnp.concatenate([b0, zer2], axis=1), jnp.concatenate([zer2, b1], axis=1)], axis=0)

    def side_by_side(xs):
        return jnp.concatenate([x.astype(BF16) for x in xs], axis=1) if len(xs) > 1 else xs[0].astype(BF16)

    def prep_group(pairs):
        st = []
        for i, slot in pairs:
            ra = _aligned(i * c, c)
            rb = _aligned((nc - 1 - i) * c, c)
            s = dict(slot=slot, ka=ks[pl.ds(ra, c), :], qa=qs[pl.ds(ra, c), :],
                     kz=ks[pl.ds(rb, c), :], qz=qs[pl.ds(rb, c), :])
            ka16, qa16, kz16, qz16 = (s[n].astype(BF16) for n in ("ka", "qa", "kz", "qz"))
            va16 = vs[pl.ds(ra, c), :].astype(BF16)
            vz16 = vs[pl.ds(rb, c), :].astype(BF16)
            s["bdv"] = jnp.concatenate([jnp.concatenate([va16, zer], axis=1),
                                        jnp.concatenate([zer, vz16], axis=1)], axis=0)
            s["bdk"] = jnp.concatenate([jnp.concatenate([ka16, zer], axis=1),
                                        jnp.concatenate([zer, kz16], axis=1)], axis=0)
            s["gram"] = lax.dot_general(
                jnp.concatenate([jnp.concatenate([ka16, qa16], axis=0), jnp.concatenate([kz16, qz16], axis=0)], axis=1),
                s["bdk"], NT_DIMS, preferred_element_type=F32)
            tile = gp_ref[i]
            s["brow"] = tile[0:1, :]
            grow = tile[1:2, :]
            x = jnp.concatenate([jnp.where(incl, grow, 0.0), jnp.where(eye, s["brow"], 0.0)], axis=0)
            hi, lo = _split2(x)
            s["cb"] = _dot(jnp.concatenate([hi, lo], axis=1), bd1[...])
            st.append(s)
        yield
        for s in st:
            cb, gram = s["cb"], s["gram"]
            s["cf_b"] = cb[0:c, 0:PAIR]
            s["cb_b"] = cb[0:c, PAIR:2 * PAIR]
            cpair = jnp.where(is_f, s["cf_b"], s["cb_b"])
            bpair = jnp.where(is_f, cb[c:2 * c, 0:PAIR], cb[c:2 * c, PAIR:2 * PAIR])
            s["crow"] = jnp.sum(jnp.where(eye, cpair, 0.0), axis=0, keepdims=True)
            decay = jnp.where(incl, jnp.exp(jnp.where(incl, cpair - s["crow"], 0.0)), 0.0)
            s["a"] = jnp.where(strict, bpair * gram[0:c, :] * decay, 0.0)
            s["attn16"] = (gram[c:2 * c, :] * decay).astype(BF16)

        ts = [jnp.where(eye, 1.0, 0.0) - jnp.where(xor == 1, s["a"], 0.0) for s in st]
        for lvl in range(1, 6):
            es = [jnp.where((xor >> lvl) == 1, s["a"], 0.0) for s in st]
            te = _dot(side_by_side(ts), blockdiag_many(es))
            yield
            tet = _dot(te.astype(BF16), blockdiag_many(ts))
            yield
            ts = [t - tet[:, n * PAIR:(n + 1) * PAIR] for n, t in enumerate(ts)]

        for s, t in zip(st, ts):
            erow = jnp.exp(s["crow"])
            s["u"] = _dot((t * s["brow"]).astype(BF16), s["bdv"])
            s["w"] = _dot((t * (s["brow"] * erow)).astype(BF16), s["bdk"])
        yield

        for s in st:
            s["last_f"] = s["cf_b"][c - 1:c, :]
            s["last_b"] = s["cb_b"][0:1, :]
            s["dirs"] = ((0, s["ka"], s["qa"], s["cf_b"], s["last_f"], is_f, pq_f, n_f, oc_f),
                         (LANES, s["kz"], s["qz"], s["cb_b"], s["last_b"], is_b, pq_b, n_b, oc_b))
            s["prods"] = []
            for (lo, kx, qx, cx, last, keep, pq, nn_ref, oc) in s["dirs"]:
                uw = jnp.concatenate([s["u"][:, lo:lo + LANES], s["w"][:, lo:lo + LANES]], axis=1).astype(BF16)
                kdec = (kx * jnp.exp(last - cx)).astype(BF16)
                n_p = lax.dot_general(kdec, uw, TN_DIMS, preferred_element_type=F32)
                o_aw = _dot(jnp.where(keep, s["attn16"], zer), jnp.concatenate([uw, uw], axis=0))
                s["prods"].append((n_p, o_aw))
        yield
        for s in st:
            slot = s["slot"]
            p0 = _aligned(slot * (3 * c), c)
            n0 = _aligned(slot * (2 * c), 2 * c)
            o0 = _aligned(slot * c, c)
            for (lo, kx, qx, cx, last, keep, pq, nn_ref, oc), (n_p, o_aw) in zip(s["dirs"], s["prods"]):
                pq[pl.ds(p0, 2 * c), :] = n_p[:, LANES:2 * LANES].astype(BF16)
                pq[pl.ds(p0 + 2 * c, c), :] = (qx * jnp.exp(cx) - o_aw[:, LANES:2 * LANES]).astype(BF16)
                nn_ref[pl.ds(n0, 2 * c), :] = n_p[:, 0:LANES]
                oc[pl.ds(o0, c), :] = o_aw[:, 0:LANES]
            el_scr[slot] = jnp.concatenate([jnp.exp(s["last_f"]), jnp.exp(s["last_b"]),
                                            jnp.zeros((SUBLANES - 2, LANES), F32)], axis=0)

    def rec_block(blk, state):
        s_f, s_b = state
        for j in range(unroll):
            i = blk * unroll + j
            slot = (blk & 1) * unroll + j
            p0 = _aligned(slot * (3 * c), c)
            n0 = _aligned(slot * (2 * c), 2 * c)
            o0 = _aligned(slot * c, c)
            el = el_scr[slot]
            r_f = _dot(pq_f[pl.ds(p0, 3 * c), :], s_f.astype(BF16))
            r_b = _dot(pq_b[pl.ds(p0, 3 * c), :], s_b.astype(BF16))
            yield
            of[pl.ds(_aligned(i * c, c), c), :] = oc_f[pl.ds(o0, c), :] + r_f[2 * c:3 * c, :]
            s_f = s_f * el[0:1, :] + n_f[pl.ds(n0, 2 * c), :] - r_f[0:2 * c, :]
            ob[pl.ds(_aligned((nc - 1 - i) * c, c), c), :] = oc_b[pl.ds(o0, c), :] + r_b[2 * c:3 * c, :]
            s_b = s_b * el[1:2, :] + n_b[pl.ds(n0, 2 * c), :] - r_b[0:2 * c, :]
        state[0], state[1] = s_f, s_b

    def run_interleaved(gens, rec=None):
        rec_period = max(1, (PREP_STAGES * len(gens)) // (unroll + 1))
        steps = 0
        while gens or rec is not None:
            alive = []
            for g in gens:
                if _advance(g):
                    alive.append(g)
                steps += 1
                if rec is not None and steps % rec_period == 0 and not _advance(rec):
                    rec = None
            gens = alive
            if not gens and rec is not None and not _advance(rec):
                rec = None

    def prep_gens(blk):
        pairs = [(blk * unroll + j, (blk & 1) * unroll + j) for j in range(unroll)]
        return [prep_group(pairs[j:j + DN_GROUP]) for j in range(0, unroll, DN_GROUP)]

    def pipelined(blk, carry):
        state = list(carry)
        run_interleaved(prep_gens(blk), rec_block(blk - 1, state))
        return tuple(state)

    def finish_rows(rows):
        hn = hn_ref[...]
        for r0 in rows:
            for s0 in range(r0, r0 + CONV_ROWS, CONV_PIECE):
                o = of[pl.ds(s0, CONV_PIECE), :] + ob[pl.ds(s0, CONV_PIECE), :]
                zz = z_ref[pl.ds(s0, CONV_PIECE), :].astype(F32)
                o_ref[pl.ds(s0, CONV_PIECE), :] = (_rms(o, hn) * (zz / (1.0 + jnp.exp(-zz)))).astype(BF16)
            yield

    run_interleaved(prep_gens(0) + [conv_rows(mid_rows)])
    s0 = jnp.zeros((LANES, LANES), F32)
    state = list(lax.fori_loop(1, nblk, pipelined, (s0, s0)))
    run_interleaved([finish_rows(mid_rows)] if nblk > 1 else [], rec_block(nblk - 1, state))
    for _ in finish_rows(edge_rows if nblk > 1 else all_rows):
        pass


DN_UNROLL = 16
DN_GROUP = 1
DN_MIN_BLOCKS = 4


def _deltanet(p, cw, gp, hn, *, batch, seq):
    assert seq % CONV_ROWS == 0 and CONV_ROWS % CHUNK == 0, (seq, CONV_ROWS)
    nc = seq // CHUNK
    unroll = math.gcd(nc, DN_UNROLL)
    while unroll > 1 and nc // unroll < DN_MIN_BLOCKS:
        unroll //= 2
    h = DN_HEADS
    slots = 2 * unroll
    c = CHUNK
    col = lambda off: pl.BlockSpec((None, seq, LANES), lambda b, hd, off=off: (b, 0, off + hd))
    cws = lambda off: pl.BlockSpec((SUBLANES, LANES), lambda b, hd, off=off: (0, off + hd))
    return pl.pallas_call(
        functools.partial(_deltanet_body, seq=seq, unroll=unroll),
        grid=(batch, h),
        in_specs=[col(0), col(h), col(2 * h), col(3 * h), cws(0), cws(h), cws(2 * h),
                  pl.BlockSpec((nc, SUBLANES, LANES), lambda b, hd: (b, hd, 0)),
                  pl.BlockSpec((1, LANES), lambda b, hd: (0, 0))],
        out_specs=pl.BlockSpec((None, seq, LANES), lambda b, hd: (b, 0, hd)),
        out_shape=jax.ShapeDtypeStruct((batch, seq, h * LANES), BF16),
        scratch_shapes=[
            pltpu.VMEM((seq + 2 * HALO, LANES), F32),
            pltpu.VMEM((seq + 2 * HALO, LANES), F32),
            pltpu.VMEM((seq + 2 * HALO, LANES), F32),
            pltpu.VMEM((seq, LANES), F32),
            pltpu.VMEM((seq, LANES), F32),
            pltpu.VMEM((seq, LANES), F32),
            pltpu.VMEM((slots * 3 * c, LANES), BF16),
            pltpu.VMEM((slots * 3 * c, LANES), BF16),
            pltpu.VMEM((slots * 2 * c, LANES), F32),
            pltpu.VMEM((slots * 2 * c, LANES), F32),
            pltpu.VMEM((slots * c, LANES), F32),
            pltpu.VMEM((slots * c, LANES), F32),
            pltpu.VMEM((slots, SUBLANES, LANES), F32),
            pltpu.VMEM((seq, LANES), F32),
            pltpu.VMEM((seq, LANES), F32),
            pltpu.VMEM((2 * PAIR, 2 * PAIR), BF16),
        ],
        compiler_params=_params("parallel", "parallel"),
        name="deltanet",
    )(p, p, p, p, cw, cw, cw, gp, hn)


TW_ROWS = 64


def _twiddle_body(ac_ref, as_ref, bc_ref, bs_ref, c_ref, s_ref):
    ac = ac_ref[...]
    sn = as_ref[...]
    bc = bc_ref[...]
    bs = bs_ref[...]
    c_ref[...] = (ac * bc - sn * bs).astype(BF16)
    s_ref[...] = (-(sn * bc + ac * bs)).astype(BF16)


def _twiddle(seq, tm):
    nblk = seq // 2 // tm
    rows = tm + SUBLANES
    kk = jnp.arange(seq, dtype=jnp.int32)[None, :]
    step = 2.0 * math.pi / seq
    ang_a = ((jnp.arange(rows, dtype=jnp.int32)[:, None] * kk) % seq).astype(F32) * step
    ang_b = ((jnp.arange(nblk, dtype=jnp.int32)[:, None] * tm * kk) % seq).astype(F32) * step
    tab = pl.BlockSpec((rows, seq), lambda i: (0, 0))
    row = pl.BlockSpec((None, 1, seq), lambda i: (i, 0, 0))
    out = pl.BlockSpec((None, rows, seq), lambda i: (i, 0, 0))
    return pl.pallas_call(
        _twiddle_body,
        grid=(nblk,),
        in_specs=[tab, tab, row, row],
        out_specs=[out, out],
        out_shape=[jax.ShapeDtypeStruct((nblk, rows, seq), BF16)] * 2,
        compiler_params=_params("parallel"),
        name="twiddle",
    )(jnp.cos(ang_a), jnp.sin(ang_a), jnp.cos(ang_b)[:, None, :], jnp.sin(ang_b)[:, None, :])


def _chandft_body(u_ref, w_ref, rc_ref, rs_ref):
    gd = w_ref.shape[0]
    w = w_ref[...]
    for g in range(FN_GROUPS):
        y = _dot(u_ref[:, g * gd:(g + 1) * gd], w)
        rc_ref[:, g * gd:(g + 1) * gd] = y[:, 0:gd].astype(BF16)
        rs_ref[:, g * gd:(g + 1) * gd] = y[:, gd:2 * gd].astype(BF16)


def _chandft(p, wch, *, col_block, tm=512):
    t = p.shape[0]
    dfn = FN_GROUPS * wch.shape[0]
    out = pl.BlockSpec((tm, dfn), lambda i: (i, 0))
    return pl.pallas_call(
        _chandft_body,
        grid=(t // tm,),
        in_specs=[pl.BlockSpec((tm, dfn), lambda i: (i, col_block)),
                  pl.BlockSpec(wch.shape, lambda i: (0, 0))],
        out_specs=[out, out],
        out_shape=[jax.ShapeDtypeStruct((t, dfn), BF16)] * 2,
        compiler_params=_params("parallel"),
        name="chandft",
    )(p, wch)


def _seqdft_body(c_ref, s_ref, rc_ref, rs_ref, o_ref, hi_scr):
    tm = o_ref.shape[0]
    rows = c_ref.shape[0]

    @pl.when(pl.program_id(2) == 0)
    def _():
        p = _dot(c_ref[...], rc_ref[...])
        mq = _dot(s_ref[...], rs_ref[...])
        o_ref[...] = (p[0:tm, :] + mq[0:tm, :]).astype(BF16)
        jj = lax.broadcasted_iota(jnp.int32, (tm, rows), 0)
        mm = lax.broadcasted_iota(jnp.int32, (tm, rows), 1)
        pick = jnp.where(jj + mm == tm, 1.0, 0.0).astype(BF16)
        hi_scr[...] = _dot(pick, (p - mq).astype(BF16)).astype(BF16)

    @pl.when(pl.program_id(2) == 1)
    def _():
        o_ref[...] = hi_scr[...]


SEQDFT_ROWS = 256


def _seqdft(ctw, stw, rc, rs, *, batch, seq):
    dfn = rc.shape[-1]
    nblk, rows, _ = ctw.shape
    tm = rows - SUBLANES
    tw = pl.BlockSpec((None, rows, seq), lambda b, i, h: (i, 0, 0))
    rhs = pl.BlockSpec((None, seq, dfn), lambda b, i, h: (b, 0, 0))
    return pl.pallas_call(
        _seqdft_body,
        grid=(batch, nblk, 2),
        in_specs=[tw, tw, rhs, rhs],
        out_specs=pl.BlockSpec((None, tm, dfn),
                               lambda b, i, h: (b, i + h * (2 * nblk - 1 - 2 * i), 0)),
        out_shape=jax.ShapeDtypeStruct((batch, seq, dfn), BF16),
        scratch_shapes=[pltpu.VMEM((tm, dfn), BF16)],
        compiler_params=_params("parallel", "arbitrary", "arbitrary"),
        name="seqdft",
    )(ctw, stw, rc.reshape(batch, seq, dfn), rs.reshape(batch, seq, dfn))


def _outproj_body(x_ref, a_ref, b_ref, wa_ref, wb_ref, o_ref):
    o_ref[...] = x_ref[...] + _dot(a_ref[...], wa_ref[...]) + _dot(b_ref[...], wb_ref[...])


def _outproj(x, a, b, wa, wb, *, tm=512):
    t, d = x.shape
    da = a.shape[1]
    db = b.shape[1]
    return pl.pallas_call(
        _outproj_body,
        grid=(t // tm,),
        in_specs=[pl.BlockSpec((tm, d), lambda i: (i, 0)),
                  pl.BlockSpec((tm, da), lambda i: (i, 0)),
                  pl.BlockSpec((tm, db), lambda i: (i, 0)),
                  pl.BlockSpec((da, d), lambda i: (0, 0)),
                  pl.BlockSpec((db, d), lambda i: (0, 0))],
        out_specs=pl.BlockSpec((tm, d), lambda i: (i, 0)),
        out_shape=jax.ShapeDtypeStruct((t, d), F32),
        compiler_params=_params("parallel"),
        name="outproj",
    )(x, a, b, wa, wb)


def _kvproj_body(m_ref, g_ref, w_ref, o_ref, h_ref):
    @pl.when(pl.program_id(1) == 0)
    def _():
        h_ref[...] = _rms(m_ref[...], g_ref[...]).astype(BF16)

    o_ref[...] = _dot(h_ref[...], w_ref[...]).astype(BF16)


def _kvproj(mem, g, w, *, tn=1024):
    b, n, d = mem.shape
    nout = w.shape[1]
    return pl.pallas_call(
        _kvproj_body,
        grid=(b, nout // tn),
        in_specs=[pl.BlockSpec((None, n, d), lambda i, j: (i, 0, 0)),
                  pl.BlockSpec((1, d), lambda i, j: (0, 0)),
                  pl.BlockSpec((d, tn), lambda i, j: (0, j))],
        out_specs=pl.BlockSpec((None, n, tn), lambda i, j: (i, 0, j)),
        out_shape=jax.ShapeDtypeStruct((b, n, nout), BF16),
        scratch_shapes=[pltpu.VMEM((n, d), BF16)],
        compiler_params=_params("parallel", "arbitrary"),
        name="kvproj",
    )(mem, g, w)


def _xattn_body(x_ref, g_ref, wq_ref, kv_ref, wo_ref, o_ref, q_scr, a_scr):
    d = x_ref.shape[-1]
    hd = d // XA_HEADS
    x = x_ref[...]
    q_scr[...] = _dot(_rms(x, g_ref[...]).astype(BF16), wq_ref[...]).astype(BF16)
    for h in range(XA_HEADS):
        kh = kv_ref[:, h * hd:(h + 1) * hd]
        vh = kv_ref[:, d + h * hd:d + (h + 1) * hd]
        s = lax.dot_general(q_scr[:, h * hd:(h + 1) * hd], kh, NT_DIMS,
                            preferred_element_type=F32) * (hd ** -0.5)
        p = jnp.exp(s - jnp.max(s, axis=-1, keepdims=True))
        l = jnp.sum(p, axis=-1, keepdims=True)
        a_scr[:, h * hd:(h + 1) * hd] = (_dot(p.astype(BF16), vh) / l).astype(BF16)
    o_ref[...] = x + _dot(a_scr[...], wo_ref[...])


def _xattn(x, g, wq, kv, wo, *, batch, seq, tm=256):
    d = x.shape[-1]
    n = kv.shape[1]
    return pl.pallas_call(
        _xattn_body,
        grid=(batch, seq // tm),
        in_specs=[pl.BlockSpec((None, tm, d), lambda b, i: (b, i, 0)),
                  pl.BlockSpec((1, d), lambda b, i: (0, 0)),
                  pl.BlockSpec((d, d), lambda b, i: (0, 0)),
                  pl.BlockSpec((None, n, 2 * d), lambda b, i: (b, 0, 0)),
                  pl.BlockSpec((d, d), lambda b, i: (0, 0))],
        out_specs=pl.BlockSpec((None, tm, d), lambda b, i: (b, i, 0)),
        out_shape=jax.ShapeDtypeStruct((batch, seq, d), F32),
        scratch_shapes=[pltpu.VMEM((tm, d), BF16), pltpu.VMEM((tm, d), BF16)],
        compiler_params=_params("parallel", "arbitrary"),
        name="xattn",
    )(x.reshape(batch, seq, d), g, wq, kv, wo)


def _prep_weights(ffn1_norm, ffn1_w_gate, ffn1_w_up, ffn1_w_down, mix_norm, w_in, conv_w, a_log, dt_bias,
                  dn_head_norm, w_out, xattn_norm, mem_norm, xattn_w_q, xattn_w_kv, xattn_w_o, ffn2_norm,
                  ffn2_w_gate, ffn2_w_up, ffn2_w_down, final_norm, *, tm_inproj):
    d = w_in.shape[0]
    dd = DN_HEADS * DN_HEAD_DIM

    def ffn_w(wg, wu, wd):
        return (wg.astype(BF16), wu.astype(BF16), wd.astype(BF16))

    row = lambda v: v.reshape(1, -1).astype(F32)
    off = 4 * dd
    w_main = jnp.concatenate([w_in[:, :off], w_in[:, off + 4 * DN_HEADS:]], axis=1).astype(BF16)
    beta_w = w_in[:, off:off + 2 * DN_HEADS].reshape(d, 2, DN_HEADS)
    a_w = w_in[:, off + 2 * DN_HEADS:off + 4 * DN_HEADS].reshape(d, 2, DN_HEADS)
    gate_w = jnp.concatenate([beta_w, a_w, jnp.zeros((d, SUBLANES - 4, DN_HEADS), F32)], axis=1)
    wgt = gate_w.transpose(2, 1, 0).reshape(DN_HEADS * SUBLANES, d).astype(BF16)

    def gate_param(v):
        full = jnp.concatenate([jnp.zeros((2, DN_HEADS), F32), v.astype(F32),
                                jnp.zeros((SUBLANES - 4, DN_HEADS), F32)], axis=0)
        return jnp.broadcast_to(full.T.reshape(DN_HEADS * SUBLANES, 1), (DN_HEADS * SUBLANES, tm_inproj))

    gd = (w_in.shape[1] - off - 4 * DN_HEADS) // FN_GROUPS
    idx = jnp.arange(gd, dtype=jnp.int32)
    ang = ((idx[:, None] * idx[None, :]) % gd).astype(F32) * (2.0 * math.pi / gd)
    return dict(
        ffn1=(row(ffn1_norm),) + ffn_w(ffn1_w_gate, ffn1_w_up, ffn1_w_down),
        ffn2=(row(ffn2_norm),) + ffn_w(ffn2_w_gate, ffn2_w_up, ffn2_w_down),
        final=row(final_norm), mix=row(mix_norm), w_main=w_main, wgt=wgt,
        alog=gate_param(a_log), dt=gate_param(dt_bias),
        conv=jnp.pad(conv_w.astype(F32), ((0, SUBLANES - CONV_WIDTH), (0, 0))),
        hn=row(dn_head_norm), chan=(jnp.cos(ang), jnp.sin(ang)),
        wout_a=w_out[:dd].astype(BF16), wout_b=w_out[dd:].astype(BF16),
        xn=row(xattn_norm), mn=row(mem_norm), wq=xattn_w_q.astype(BF16), wkv=xattn_w_kv.astype(BF16),
        wo=xattn_w_o.astype(BF16))


def _gate_pairs(gt, batch, seq):
    nc = seq // CHUNK
    g = gt.reshape(DN_HEADS, SUBLANES, batch, nc, CHUNK)
    rev = lambda a: a[:, :, ::-1, :]
    pairs = jnp.stack([jnp.concatenate([g[:, 0], rev(g[:, 1])], axis=-1),
                       jnp.concatenate([g[:, 2], rev(g[:, 3])], axis=-1)], axis=1)
    pairs = jnp.pad(pairs, ((0, 0), (0, SUBLANES - 2), (0, 0), (0, 0), (0, 0)))
    return pairs.transpose(2, 3, 0, 1, 4).reshape(batch * nc, DN_HEADS * SUBLANES, 2 * CHUNK)


def _trunk(x, mem, w, tm_inproj, final):
    batch, seq, d = x.shape
    t = batch * seq
    dd = DN_HEADS * DN_HEAD_DIM
    x0 = x.reshape(t, d)
    x1 = _ffn(x0, *w["ffn1"], w["final"], final=False)
    p, gt = _inproj(x1, w["mix"], w["w_main"], w["wgt"], w["alog"], w["dt"], tm=tm_inproj)
    gp = _gate_pairs(gt, batch, seq)
    o_dn = _deltanet(p.reshape(batch, seq, -1), w["conv"], gp, w["hn"], batch=batch, seq=seq)
    dfn = p.shape[1] - 4 * dd
    scale = (seq * (dfn // FN_GROUPS)) ** -0.5
    wch = (jnp.concatenate(w["chan"], axis=1) * scale).astype(BF16)
    rc, rs = _chandft(p, wch, col_block=4 * dd // dfn)
    ctw, stw = _twiddle(seq, min(SEQDFT_ROWS, seq // 2))
    o_fn = _seqdft(ctw, stw, rc, rs, batch=batch, seq=seq)
    x2 = _outproj(x1, o_dn.reshape(t, dd), o_fn.reshape(t, dfn), w["wout_a"], w["wout_b"])
    kv = _kvproj(mem, w["mn"], w["wkv"])
    x3 = _xattn(x2, w["xn"], w["wq"], kv, w["wo"], batch=batch, seq=seq)
    y = _ffn(x3.reshape(t, d), *w["ffn2"], w["final"], final=final)
    return y.reshape(batch, seq, d)


def kernel(x_prompt, x_sample, mem_prompt, mem_sample, ffn1_norm, ffn1_w_gate, ffn1_w_up, ffn1_w_down, mix_norm,
           w_in, conv_w, a_log, dt_bias, dn_head_norm, w_out, xattn_norm, mem_norm, xattn_w_q, xattn_w_kv,
           xattn_w_o, ffn2_norm, ffn2_w_gate, ffn2_w_up, ffn2_w_down, final_norm):
    depth = ffn1_norm.shape[0]
    tm_inproj = 512
    y_prompt, y_sample = x_prompt, x_sample
    for l in range(depth):
        w = _prep_weights(ffn1_norm[l], ffn1_w_gate[l], ffn1_w_up[l], ffn1_w_down[l], mix_norm[l], w_in[l],
                          conv_w[l], a_log[l], dt_bias[l], dn_head_norm[l], w_out[l], xattn_norm[l], mem_norm[l],
                          xattn_w_q[l], xattn_w_kv[l], xattn_w_o[l], ffn2_norm[l], ffn2_w_gate[l], ffn2_w_up[l],
                          ffn2_w_down[l], final_norm, tm_inproj=tm_inproj)
        y_prompt = _trunk(y_prompt, mem_prompt, w, tm_inproj, l == depth - 1)
        y_sample = _trunk(y_sample, mem_sample, w, tm_inproj, l == depth - 1)
    return (y_prompt, y_sample)
```

```python
import functools
import math

import jax
import jax.numpy as jnp
from jax import lax
from jax.experimental import pallas as pl
from jax.experimental.pallas import tpu as pltpu

F32 = jnp.float32
BF16 = jnp.bfloat16
EPS = 1e-6

DN_HEADS = 8
DN_HEAD_DIM = 128
FN_GROUPS = 4
CONV_WIDTH = 5
CHUNK = 64
XA_HEADS = 4
N_MEM = 256

LANES = 128
SUBLANES = 8
VMEM_LIMIT = 56 * 1024 * 1024

NT_DIMS = (((1,), (1,)), ((), ()))
TN_DIMS = (((0,), (0,)), ((), ()))


def _params(*sem):
    return pltpu.CompilerParams(dimension_semantics=sem, vmem_limit_bytes=VMEM_LIMIT)


def _dot(a, b):
    return jnp.dot(a, b, preferred_element_type=F32)


def _rms(x, g):
    ms = jnp.mean(x * x, axis=-1, keepdims=True)
    return x * lax.rsqrt(ms + EPS) * g


def _sigmoid(x):
    return 1.0 / (1.0 + jnp.exp(-x))


def _ffn_body(x_ref, g_ref, wg_ref, wu_ref, wd_ref, fin_ref, o_ref, h_ref, *, final, last_width):
    j = pl.program_id(1)
    nj = pl.num_programs(1)
    tf = wg_ref.shape[1]

    @pl.when(j == 0)
    def _():
        h_ref[...] = _rms(x_ref[...], g_ref[...]).astype(BF16)
        o_ref[...] = jnp.zeros_like(o_ref)

    def hidden_block(width):
        h = h_ref[...]
        a = _dot(h, wg_ref[:, 0:width])
        u = _dot(h, wu_ref[:, 0:width])
        act = (a * _sigmoid(a) * u).astype(BF16)
        o_ref[...] += _dot(act, wd_ref[0:width, :])

    if last_width == tf:
        hidden_block(tf)
    else:
        pl.when(j < nj - 1)(lambda: hidden_block(tf))
        pl.when(j == nj - 1)(lambda: hidden_block(last_width))

    @pl.when(j == nj - 1)
    def _():
        y = x_ref[...] + 0.5 * o_ref[...]
        if final:
            y = _rms(y, fin_ref[...])
        o_ref[...] = y


def _ffn(x, g, wg, wu, wd, fin, *, final, tm=1024, tf=512):
    t, d = x.shape
    tm = min(tm, t)
    assert t % tm == 0, (t, tm)
    dff = wg.shape[1]
    nj = pl.cdiv(dff, tf)
    last_width = dff - (nj - 1) * tf
    assert last_width % LANES == 0, (dff, tf)
    return pl.pallas_call(
        functools.partial(_ffn_body, final=final, last_width=last_width),
        grid=(t // tm, nj),
        in_specs=[
            pl.BlockSpec((tm, d), lambda i, j: (i, 0)),
            pl.BlockSpec((1, d), lambda i, j: (0, 0)),
            pl.BlockSpec((d, tf), lambda i, j: (0, j)),
            pl.BlockSpec((d, tf), lambda i, j: (0, j)),
            pl.BlockSpec((tf, d), lambda i, j: (j, 0)),
            pl.BlockSpec((1, d), lambda i, j: (0, 0)),
        ],
        out_specs=pl.BlockSpec((tm, d), lambda i, j: (i, 0)),
        out_shape=jax.ShapeDtypeStruct((t, d), F32),
        scratch_shapes=[pltpu.VMEM((tm, d), BF16)],
        compiler_params=_params("parallel", "arbitrary"),
        name="ffn_final" if final else "ffn",
    )(x, g, wg, wu, wd, fin)


def _inproj_body(x_ref, g_ref, w_ref, wgt_ref, alog_ref, dt_ref, p_ref, gt_ref, h_ref):
    j = pl.program_id(1)

    @pl.when(j == 0)
    def _():
        h = _rms(x_ref[...], g_ref[...]).astype(BF16)
        h_ref[...] = h
        gt = lax.dot_general(wgt_ref[...], h, NT_DIMS, preferred_element_type=F32)
        kind = lax.broadcasted_iota(jnp.int32, gt.shape, 0) & (SUBLANES - 1)
        xs = gt + dt_ref[...]
        softplus = jnp.maximum(xs, 0.0) + jnp.log1p(jnp.exp(-jnp.abs(xs)))
        log_decay = -jnp.exp(alog_ref[...]) * softplus
        gt_ref[...] = jnp.where(kind < 2, _sigmoid(gt), jnp.where(kind < 4, log_decay, 0.0))

    p_ref[...] = _dot(h_ref[...], w_ref[...]).astype(BF16)


def _inproj(x, g, w, wgt, alog, dt, *, tm=512, tn=2560):
    t, d = x.shape
    n = w.shape[1]
    rows = wgt.shape[0]
    return pl.pallas_call(
        _inproj_body,
        grid=(t // tm, n // tn),
        in_specs=[
            pl.BlockSpec((tm, d), lambda i, j: (i, 0)),
            pl.BlockSpec((1, d), lambda i, j: (0, 0)),
            pl.BlockSpec((d, tn), lambda i, j: (0, j)),
            pl.BlockSpec((rows, d), lambda i, j: (0, 0)),
            pl.BlockSpec((rows, tm), lambda i, j: (0, 0)),
            pl.BlockSpec((rows, tm), lambda i, j: (0, 0)),
        ],
        out_specs=[
            pl.BlockSpec((tm, tn), lambda i, j: (i, j)),
            pl.BlockSpec((rows, tm), lambda i, j: (0, i)),
        ],
        out_shape=[jax.ShapeDtypeStruct((t, n), BF16), jax.ShapeDtypeStruct((rows, t), F32)],
        scratch_shapes=[pltpu.VMEM((tm, d), BF16)],
        compiler_params=_params("parallel", "arbitrary"),
        name="inproj",
    )(x, g, w, wgt, alog, dt)


CONV_ROWS = 512
CONV_PIECE = 128
HALO = SUBLANES
PAIR = 2 * CHUNK


def _split2(x):
    hi = x.astype(BF16)
    lo = (x - hi.astype(F32)).astype(BF16)
    return hi, lo


def _aligned(x, m):
    return x if isinstance(x, int) else pl.multiple_of(x, m)


def _advance(gen):
    try:
        next(gen)
        return True
    except StopIteration:
        return False


PREP_STAGES = 14


def _deltanet_body(q_ref, k_ref, v_ref, z_ref, cwq_ref, cwk_ref, cwv_ref, gp_ref, hn_ref, o_ref,
                   upq, upk, upv, qs, ks, vs, pq_f, pq_b, n_f, n_b, oc_f, oc_b, el_scr, of, ob, bd1,
                   *, seq, unroll):
    nc = seq // CHUNK
    c = CHUNK
    nblk = nc // unroll

    streams = ((q_ref, cwq_ref, upq, qs, DN_HEAD_DIM ** -0.5), (k_ref, cwk_ref, upk, ks, 1.0),
               (v_ref, cwv_ref, upv, vs, None))
    for src_ref, _, up, _, _ in streams:
        zeros = jnp.zeros((HALO, LANES), F32)
        up[0:HALO, :] = zeros
        up[seq + HALO:seq + 2 * HALO, :] = zeros

        def fill(r, _, src_ref=src_ref, up=up):
            r0 = pl.multiple_of(r * CONV_ROWS, CONV_ROWS)
            up[pl.ds(r0 + HALO, CONV_ROWS), :] = src_ref[pl.ds(r0, CONV_ROWS), :].astype(F32)
            return 0

        lax.fori_loop(0, seq // CONV_ROWS, fill, 0)

    def conv_block(r0, cw_ref, up, dst_ref, scale):
        cw = cw_ref[...]
        for s0 in range(r0, r0 + CONV_ROWS, CONV_PIECE):
            y = None
            for tap in range(CONV_WIDTH):
                term = cw[tap:tap + 1, :] * up[pl.ds(s0 + (HALO + tap - CONV_WIDTH // 2), CONV_PIECE), :]
                y = term if y is None else y + term
            y = y / (1.0 + jnp.exp(-y))
            if scale is not None:
                y = y * lax.rsqrt(jnp.sum(y * y, axis=-1, keepdims=True) + EPS)
                if scale != 1.0:
                    y = y * scale
            dst_ref[pl.ds(s0, CONV_PIECE), :] = y

    def conv_rows(rows):
        for r0 in rows:
            for _, cw_ref, up, dst_ref, scale in streams:
                conv_block(r0, cw_ref, up, dst_ref, scale)
                yield

    edge = unroll * c
    all_rows = range(0, seq, CONV_ROWS)
    edge_rows = [r for r in all_rows if r < edge or r + CONV_ROWS > seq - edge]
    mid_rows = [r for r in all_rows if r not in edge_rows]
    for _ in conv_rows(edge_rows):
        pass

    ii = lax.broadcasted_iota(jnp.int32, (c, PAIR), 0)
    ll = lax.broadcasted_iota(jnp.int32, (c, PAIR), 1)
    jj = ll & (c - 1)
    is_f = ll < c
    ahead = jnp.where(is_f, ii - jj, jj - ii)
    incl = ahead >= 0
    strict = ahead > 0
    eye = ii == jj
    xor = ii ^ jj

    rr = lax.broadcasted_iota(jnp.int32, (2 * PAIR, 2 * PAIR), 0)
    nn = lax.broadcasted_iota(jnp.int32, (2 * PAIR, 2 * PAIR), 1)
    bd1[...] = jnp.where(((rr & (PAIR - 1)) >> (c.bit_length() - 1)) == (nn >> (PAIR.bit_length() - 1)),
                         1.0, 0.0).astype(BF16)

    def blockdiag(xp):
        top = jnp.where(is_f, xp, 0.0)
        bot = jnp.where(is_f, 0.0, xp)
        return jnp.concatenate([top, bot], axis=0).astype(BF16)

    is_b = ll >= c
    zer = jnp.zeros((c, LANES), BF16)

    def prep_pair(i, slot):
        st = []
        for i, slot in ((i, slot),):
            ra = _aligned(i * c, c)
            rb = _aligned((nc - 1 - i) * c, c)
            s = dict(slot=slot, ka=ks[pl.ds(ra, c), :], qa=qs[pl.ds(ra, c), :],
                     kz=ks[pl.ds(rb, c), :], qz=qs[pl.ds(rb, c), :])
            ka16, qa16, kz16, qz16 = (s[n].astype(BF16) for n in ("ka", "qa", "kz", "qz"))
            va16 = vs[pl.ds(ra, c), :].astype(BF16)
            vz16 = vs[pl.ds(rb, c), :].astype(BF16)
            s["bdv"] = jnp.concatenate([jnp.concatenate([va16, zer], axis=1),
                                        jnp.concatenate([zer, vz16], axis=1)], axis=0)
            s["bdk"] = jnp.concatenate([jnp.concatenate([ka16, zer], axis=1),
                                        jnp.concatenate([zer, kz16], axis=1)], axis=0)
            s["gram"] = lax.dot_general(
                jnp.concatenate([jnp.concatenate([ka16, qa16], axis=0), jnp.concatenate([kz16, qz16], axis=0)], axis=1),
                s["bdk"], NT_DIMS, preferred_element_type=F32)
            tile = gp_ref[i]
            s["brow"] = tile[0:1, :]
            grow = tile[1:2, :]
            x = jnp.concatenate([jnp.where(incl, grow, 0.0), jnp.where(eye, s["brow"], 0.0)], axis=0)
            hi, lo = _split2(x)
            s["cb"] = _dot(jnp.concatenate([hi, lo], axis=1), bd1[...])
            st.append(s)
        yield
        for s in st:
            cb, gram = s["cb"], s["gram"]
            s["cf_b"] = cb[0:c, 0:PAIR]
            s["cb_b"] = cb[0:c, PAIR:2 * PAIR]
            cpair = jnp.where(is_f, s["cf_b"], s["cb_b"])
            bpair = jnp.where(is_f, cb[c:2 * c, 0:PAIR], cb[c:2 * c, PAIR:2 * PAIR])
            s["crow"] = jnp.sum(jnp.where(eye, cpair, 0.0), axis=0, keepdims=True)
            decay = jnp.where(incl, jnp.exp(jnp.where(incl, cpair - s["crow"], 0.0)), 0.0)
            s["a"] = jnp.where(strict, bpair * gram[0:c, :] * decay, 0.0)
            s["attn16"] = (gram[c:2 * c, :] * decay).astype(BF16)

        ts = [jnp.where(eye, 1.0, 0.0) - jnp.where(xor == 1, s["a"], 0.0) for s in st]
        for lvl in range(1, 6):
            es = [jnp.where((xor >> lvl) == 1, s["a"], 0.0) for s in st]
            te = _dot(ts[0].astype(BF16), blockdiag(es[0]))
            yield
            tet = _dot(te.astype(BF16), blockdiag(ts[0]))
            yield
            ts = [t - tet[:, n * PAIR:(n + 1) * PAIR] for n, t in enumerate(ts)]

        for s, t in zip(st, ts):
            erow = jnp.exp(s["crow"])
            s["u"] = _dot((t * s["brow"]).astype(BF16), s["bdv"])
            s["w"] = _dot((t * (s["brow"] * erow)).astype(BF16), s["bdk"])
        yield

        for s in st:
            s["last_f"] = s["cf_b"][c - 1:c, :]
            s["last_b"] = s["cb_b"][0:1, :]
            s["dirs"] = ((0, s["ka"], s["qa"], s["cf_b"], s["last_f"], is_f, pq_f, n_f, oc_f),
                         (LANES, s["kz"], s["qz"], s["cb_b"], s["last_b"], is_b, pq_b, n_b, oc_b))
            s["prods"] = []
            for (lo, kx, qx, cx, last, keep, pq, nn_ref, oc) in s["dirs"]:
                uw = jnp.concatenate([s["u"][:, lo:lo + LANES], s["w"][:, lo:lo + LANES]], axis=1).astype(BF16)
                kdec = (kx * jnp.exp(last - cx)).astype(BF16)
                n_p = lax.dot_general(kdec, uw, TN_DIMS, preferred_element_type=F32)
                o_aw = _dot(jnp.where(keep, s["attn16"], zer), jnp.concatenate([uw, uw], axis=0))
                s["prods"].append((n_p, o_aw))
        yield
        for s in st:
            slot = s["slot"]
            p0 = _aligned(slot * (3 * c), c)
            n0 = _aligned(slot * (2 * c), 2 * c)
            o0 = _aligned(slot * c, c)
            for (lo, kx, qx, cx, last, keep, pq, nn_ref, oc), (n_p, o_aw) in zip(s["dirs"], s["prods"]):
                pq[pl.ds(p0, 2 * c), :] = n_p[:, LANES:2 * LANES].astype(BF16)
                pq[pl.ds(p0 + 2 * c, c), :] = (qx * jnp.exp(cx) - o_aw[:, LANES:2 * LANES]).astype(BF16)
                nn_ref[pl.ds(n0, 2 * c), :] = n_p[:, 0:LANES]
                oc[pl.ds(o0, c), :] = o_aw[:, 0:LANES]
            el_scr[slot] = jnp.concatenate([jnp.exp(s["last_f"]), jnp.exp(s["last_b"]),
                                            jnp.zeros((SUBLANES - 2, LANES), F32)], axis=0)

    def rec_block(blk, state):
        s_f, s_b = state
        for j in range(unroll):
            i = blk * unroll + j
            slot = (blk & 1) * unroll + j
            p0 = _aligned(slot * (3 * c), c)
            n0 = _aligned(slot * (2 * c), 2 * c)
            o0 = _aligned(slot * c, c)
            el = el_scr[slot]
            r_f = _dot(pq_f[pl.ds(p0, 3 * c), :], s_f.astype(BF16))
            r_b = _dot(pq_b[pl.ds(p0, 3 * c), :], s_b.astype(BF16))
            yield
            of[pl.ds(_aligned(i * c, c), c), :] = oc_f[pl.ds(o0, c), :] + r_f[2 * c:3 * c, :]
            s_f = s_f * el[0:1, :] + n_f[pl.ds(n0, 2 * c), :] - r_f[0:2 * c, :]
            ob[pl.ds(_aligned((nc - 1 - i) * c, c), c), :] = oc_b[pl.ds(o0, c), :] + r_b[2 * c:3 * c, :]
            s_b = s_b * el[1:2, :] + n_b[pl.ds(n0, 2 * c), :] - r_b[0:2 * c, :]
        state[0], state[1] = s_f, s_b

    def run_interleaved(gens, rec=None):
        rec_period = max(1, (PREP_STAGES * len(gens)) // (unroll + 1))
        steps = 0
        while gens or rec is not None:
            alive = []
            for g in gens:
                if _advance(g):
                    alive.append(g)
                steps += 1
                if rec is not None and steps % rec_period == 0 and not _advance(rec):
                    rec = None
            gens = alive
            if not gens and rec is not None and not _advance(rec):
                rec = None

    def prep_gens(blk):
        return [prep_pair(blk * unroll + j, (blk & 1) * unroll + j) for j in range(unroll)]

    def pipelined(blk, carry):
        state = list(carry)
        run_interleaved(prep_gens(blk), rec_block(blk - 1, state))
        return tuple(state)

    def finish_rows(rows):
        hn = hn_ref[...]
        for r0 in rows:
            for s0 in range(r0, r0 + CONV_ROWS, CONV_PIECE):
                o = of[pl.ds(s0, CONV_PIECE), :] + ob[pl.ds(s0, CONV_PIECE), :]
                zz = z_ref[pl.ds(s0, CONV_PIECE), :].astype(F32)
                o_ref[pl.ds(s0, CONV_PIECE), :] = (_rms(o, hn) * (zz / (1.0 + jnp.exp(-zz)))).astype(BF16)
            yield

    run_interleaved(prep_gens(0) + [conv_rows(mid_rows)])
    s0 = jnp.zeros((LANES, LANES), F32)
    state = list(lax.fori_loop(1, nblk, pipelined, (s0, s0)))
    run_interleaved([finish_rows(mid_rows)] if nblk > 1 else [], rec_block(nblk - 1, state))
    for _ in finish_rows(edge_rows if nblk > 1 else all_rows):
        pass


DN_UNROLL = 16
DN_MIN_BLOCKS = 4


def _deltanet(p, cw, gp, hn, *, batch, seq):
    assert seq % CONV_ROWS == 0 and CONV_ROWS % CHUNK == 0, (seq, CONV_ROWS)
    nc = seq // CHUNK
    unroll = math.gcd(nc, DN_UNROLL)
    while unroll > 1 and nc // unroll < DN_MIN_BLOCKS:
        unroll //= 2
    h = DN_HEADS
    slots = 2 * unroll
    c = CHUNK
    col = lambda off: pl.BlockSpec((None, seq, LANES), lambda b, hd, off=off: (b, 0, off + hd))
    cws = lambda off: pl.BlockSpec((SUBLANES, LANES), lambda b, hd, off=off: (0, off + hd))
    return pl.pallas_call(
        functools.partial(_deltanet_body, seq=seq, unroll=unroll),
        grid=(batch, h),
        in_specs=[col(0), col(h), col(2 * h), col(3 * h), cws(0), cws(h), cws(2 * h),
                  pl.BlockSpec((nc, SUBLANES, LANES), lambda b, hd: (b, hd, 0)),
                  pl.BlockSpec((1, LANES), lambda b, hd: (0, 0))],
        out_specs=pl.BlockSpec((None, seq, LANES), lambda b, hd: (b, 0, hd)),
        out_shape=jax.ShapeDtypeStruct((batch, seq, h * LANES), BF16),
        scratch_shapes=[
            pltpu.VMEM((seq + 2 * HALO, LANES), F32),
            pltpu.VMEM((seq + 2 * HALO, LANES), F32),
            pltpu.VMEM((seq + 2 * HALO, LANES), F32),
            pltpu.VMEM((seq, LANES), F32),
            pltpu.VMEM((seq, LANES), F32),
            pltpu.VMEM((seq, LANES), F32),
            pltpu.VMEM((slots * 3 * c, LANES), BF16),
            pltpu.VMEM((slots * 3 * c, LANES), BF16),
            pltpu.VMEM((slots * 2 * c, LANES), F32),
            pltpu.VMEM((slots * 2 * c, LANES), F32),
            pltpu.VMEM((slots * c, LANES), F32),
            pltpu.VMEM((slots * c, LANES), F32),
            pltpu.VMEM((slots, SUBLANES, LANES), F32),
            pltpu.VMEM((seq, LANES), F32),
            pltpu.VMEM((seq, LANES), F32),
            pltpu.VMEM((2 * PAIR, 2 * PAIR), BF16),
        ],
        compiler_params=_params("parallel", "parallel"),
        name="deltanet",
    )(p, p, p, p, cw, cw, cw, gp, hn)


TW_ROWS = 64


def _twiddle_body(ac_ref, as_ref, bc_ref, bs_ref, c_ref, s_ref):
    ac = ac_ref[...]
    sn = as_ref[...]
    bc = bc_ref[...]
    bs = bs_ref[...]
    c_ref[...] = (ac * bc - sn * bs).astype(BF16)
    s_ref[...] = (-(sn * bc + ac * bs)).astype(BF16)


def _twiddle(seq, tm):
    nblk = seq // 2 // tm
    rows = tm + SUBLANES
    kk = jnp.arange(seq, dtype=jnp.int32)[None, :]
    step = 2.0 * math.pi / seq
    ang_a = ((jnp.arange(rows, dtype=jnp.int32)[:, None] * kk) % seq).astype(F32) * step
    ang_b = ((jnp.arange(nblk, dtype=jnp.int32)[:, None] * tm * kk) % seq).astype(F32) * step
    tab = pl.BlockSpec((rows, seq), lambda i: (0, 0))
    row = pl.BlockSpec((None, 1, seq), lambda i: (i, 0, 0))
    out = pl.BlockSpec((None, rows, seq), lambda i: (i, 0, 0))
    return pl.pallas_call(
        _twiddle_body,
        grid=(nblk,),
        in_specs=[tab, tab, row, row],
        out_specs=[out, out],
        out_shape=[jax.ShapeDtypeStruct((nblk, rows, seq), BF16)] * 2,
        compiler_params=_params("parallel"),
        name="twiddle",
    )(jnp.cos(ang_a), jnp.sin(ang_a), jnp.cos(ang_b)[:, None, :], jnp.sin(ang_b)[:, None, :])


def _chandft_body(u_ref, w_ref, rc_ref, rs_ref):
    gd = w_ref.shape[0]
    w = w_ref[...]
    for g in range(FN_GROUPS):
        y = _dot(u_ref[:, g * gd:(g + 1) * gd], w)
        rc_ref[:, g * gd:(g + 1) * gd] = y[:, 0:gd].astype(BF16)
        rs_ref[:, g * gd:(g + 1) * gd] = y[:, gd:2 * gd].astype(BF16)


def _chandft(p, wch, *, col_block, tm=512):
    t = p.shape[0]
    dfn = FN_GROUPS * wch.shape[0]
    out = pl.BlockSpec((tm, dfn), lambda i: (i, 0))
    return pl.pallas_call(
        _chandft_body,
        grid=(t // tm,),
        in_specs=[pl.BlockSpec((tm, dfn), lambda i: (i, col_block)),
                  pl.BlockSpec(wch.shape, lambda i: (0, 0))],
        out_specs=[out, out],
        out_shape=[jax.ShapeDtypeStruct((t, dfn), BF16)] * 2,
        compiler_params=_params("parallel"),
        name="chandft",
    )(p, wch)


def _seqdft_body(c_ref, s_ref, rc_ref, rs_ref, o_ref, hi_scr):
    tm = o_ref.shape[0]
    rows = c_ref.shape[0]

    @pl.when(pl.program_id(2) == 0)
    def _():
        p = _dot(c_ref[...], rc_ref[...])
        mq = _dot(s_ref[...], rs_ref[...])
        o_ref[...] = (p[0:tm, :] + mq[0:tm, :]).astype(BF16)
        jj = lax.broadcasted_iota(jnp.int32, (tm, rows), 0)
        mm = lax.broadcasted_iota(jnp.int32, (tm, rows), 1)
        pick = jnp.where(jj + mm == tm, 1.0, 0.0).astype(BF16)
        hi_scr[...] = _dot(pick, (p - mq).astype(BF16)).astype(BF16)

    @pl.when(pl.program_id(2) == 1)
    def _():
        o_ref[...] = hi_scr[...]


SEQDFT_ROWS = 256


def _seqdft(ctw, stw, rc, rs, *, batch, seq):
    dfn = rc.shape[-1]
    nblk, rows, _ = ctw.shape
    tm = rows - SUBLANES
    tw = pl.BlockSpec((None, rows, seq), lambda b, i, h: (jnp.minimum(i + h, nblk - 1), 0, 0))
    rhs = pl.BlockSpec((None, seq, dfn), lambda b, i, h: (b, 0, 0))
    return pl.pallas_call(
        _seqdft_body,
        grid=(batch, nblk, 2),
        in_specs=[tw, tw, rhs, rhs],
        out_specs=pl.BlockSpec((None, tm, dfn),
                               lambda b, i, h: (b, i + h * (2 * nblk - 1 - 2 * i), 0)),
        out_shape=jax.ShapeDtypeStruct((batch, seq, dfn), BF16),
        scratch_shapes=[pltpu.VMEM((tm, dfn), BF16)],
        compiler_params=_params("parallel", "arbitrary", "arbitrary"),
        name="seqdft",
    )(ctw, stw, rc.reshape(batch, seq, dfn), rs.reshape(batch, seq, dfn))


def _outproj_body(x_ref, a_ref, b_ref, wa_ref, wb_ref, o_ref):
    o_ref[...] = x_ref[...] + _dot(a_ref[...], wa_ref[...]) + _dot(b_ref[...], wb_ref[...])


def _outproj(x, a, b, wa, wb, *, tm=512):
    t, d = x.shape
    da = a.shape[1]
    db = b.shape[1]
    return pl.pallas_call(
        _outproj_body,
        grid=(t // tm,),
        in_specs=[pl.BlockSpec((tm, d), lambda i: (i, 0)),
                  pl.BlockSpec((tm, da), lambda i: (i, 0)),
                  pl.BlockSpec((tm, db), lambda i: (i, 0)),
                  pl.BlockSpec((da, d), lambda i: (0, 0)),
                  pl.BlockSpec((db, d), lambda i: (0, 0))],
        out_specs=pl.BlockSpec((tm, d), lambda i: (i, 0)),
        out_shape=jax.ShapeDtypeStruct((t, d), F32),
        compiler_params=_params("parallel"),
        name="outproj",
    )(x, a, b, wa, wb)


def _kvproj_body(m_ref, g_ref, w_ref, o_ref, h_ref):
    @pl.when(pl.program_id(1) == 0)
    def _():
        h_ref[...] = _rms(m_ref[...], g_ref[...]).astype(BF16)

    o_ref[...] = _dot(h_ref[...], w_ref[...]).astype(BF16)


def _kvproj(mem, g, w, *, tn=1024):
    b, n, d = mem.shape
    nout = w.shape[1]
    return pl.pallas_call(
        _kvproj_body,
        grid=(b, nout // tn),
        in_specs=[pl.BlockSpec((None, n, d), lambda i, j: (i, 0, 0)),
                  pl.BlockSpec((1, d), lambda i, j: (0, 0)),
                  pl.BlockSpec((d, tn), lambda i, j: (0, j))],
        out_specs=pl.BlockSpec((None, n, tn), lambda i, j: (i, 0, j)),
        out_shape=jax.ShapeDtypeStruct((b, n, nout), BF16),
        scratch_shapes=[pltpu.VMEM((n, d), BF16)],
        compiler_params=_params("parallel", "arbitrary"),
        name="kvproj",
    )(mem, g, w)


def _xattn_body(x_ref, g_ref, wq_ref, kv_ref, wo_ref, o_ref, q_scr, a_scr):
    d = x_ref.shape[-1]
    hd = d // XA_HEADS
    x = x_ref[...]
    q_scr[...] = _dot(_rms(x, g_ref[...]).astype(BF16), wq_ref[...]).astype(BF16)
    for h in range(XA_HEADS):
        kh = kv_ref[:, h * hd:(h + 1) * hd]
        vh = kv_ref[:, d + h * hd:d + (h + 1) * hd]
        s = lax.dot_general(q_scr[:, h * hd:(h + 1) * hd], kh, NT_DIMS,
                            preferred_element_type=F32) * (hd ** -0.5)
        p = jnp.exp(s - jnp.max(s, axis=-1, keepdims=True))
        l = jnp.sum(p, axis=-1, keepdims=True)
        a_scr[:, h * hd:(h + 1) * hd] = (_dot(p.astype(BF16), vh) / l).astype(BF16)
    o_ref[...] = x + _dot(a_scr[...], wo_ref[...])


def _xattn(x, g, wq, kv, wo, *, batch, seq, tm=256):
    d = x.shape[-1]
    n = kv.shape[1]
    return pl.pallas_call(
        _xattn_body,
        grid=(batch, seq // tm),
        in_specs=[pl.BlockSpec((None, tm, d), lambda b, i: (b, i, 0)),
                  pl.BlockSpec((1, d), lambda b, i: (0, 0)),
                  pl.BlockSpec((d, d), lambda b, i: (0, 0)),
                  pl.BlockSpec((None, n, 2 * d), lambda b, i: (b, 0, 0)),
                  pl.BlockSpec((d, d), lambda b, i: (0, 0))],
        out_specs=pl.BlockSpec((None, tm, d), lambda b, i: (b, i, 0)),
        out_shape=jax.ShapeDtypeStruct((batch, seq, d), F32),
        scratch_shapes=[pltpu.VMEM((tm, d), BF16), pltpu.VMEM((tm, d), BF16)],
        compiler_params=_params("parallel", "arbitrary"),
        name="xattn",
    )(x.reshape(batch, seq, d), g, wq, kv, wo)


def _prep_weights(ffn1_norm, ffn1_w_gate, ffn1_w_up, ffn1_w_down, mix_norm, w_in, conv_w, a_log, dt_bias,
                  dn_head_norm, w_out, xattn_norm, mem_norm, xattn_w_q, xattn_w_kv, xattn_w_o, ffn2_norm,
                  ffn2_w_gate, ffn2_w_up, ffn2_w_down, final_norm, *, tm_inproj):
    d = w_in.shape[0]
    dd = DN_HEADS * DN_HEAD_DIM

    def ffn_w(wg, wu, wd):
        return (wg.astype(BF16), wu.astype(BF16), wd.astype(BF16))

    row = lambda v: v.reshape(1, -1).astype(F32)
    off = 4 * dd
    w_main = jnp.concatenate([w_in[:, :off], w_in[:, off + 4 * DN_HEADS:]], axis=1).astype(BF16)
    beta_w = w_in[:, off:off + 2 * DN_HEADS].reshape(d, 2, DN_HEADS)
    a_w = w_in[:, off + 2 * DN_HEADS:off + 4 * DN_HEADS].reshape(d, 2, DN_HEADS)
    gate_w = jnp.concatenate([beta_w, a_w, jnp.zeros((d, SUBLANES - 4, DN_HEADS), F32)], axis=1)
    wgt = gate_w.transpose(2, 1, 0).reshape(DN_HEADS * SUBLANES, d).astype(BF16)

    def gate_param(v):
        full = jnp.concatenate([jnp.zeros((2, DN_HEADS), F32), v.astype(F32),
                                jnp.zeros((SUBLANES - 4, DN_HEADS), F32)], axis=0)
        return jnp.broadcast_to(full.T.reshape(DN_HEADS * SUBLANES, 1), (DN_HEADS * SUBLANES, tm_inproj))

    gd = (w_in.shape[1] - off - 4 * DN_HEADS) // FN_GROUPS
    idx = jnp.arange(gd, dtype=jnp.int32)
    ang = ((idx[:, None] * idx[None, :]) % gd).astype(F32) * (2.0 * math.pi / gd)
    return dict(
        ffn1=(row(ffn1_norm),) + ffn_w(ffn1_w_gate, ffn1_w_up, ffn1_w_down),
        ffn2=(row(ffn2_norm),) + ffn_w(ffn2_w_gate, ffn2_w_up, ffn2_w_down),
        final=row(final_norm), mix=row(mix_norm), w_main=w_main, wgt=wgt,
        alog=gate_param(a_log), dt=gate_param(dt_bias),
        conv=jnp.pad(conv_w.astype(F32), ((0, SUBLANES - CONV_WIDTH), (0, 0))),
        hn=row(dn_head_norm), chan=(jnp.cos(ang), jnp.sin(ang)),
        wout_a=w_out[:dd].astype(BF16), wout_b=w_out[dd:].astype(BF16),
        xn=row(xattn_norm), mn=row(mem_norm), wq=xattn_w_q.astype(BF16), wkv=xattn_w_kv.astype(BF16),
        wo=xattn_w_o.astype(BF16))


def _gate_pairs(gt, batch, seq):
    nc = seq // CHUNK
    g = gt.reshape(DN_HEADS, SUBLANES, batch, nc, CHUNK)
    rev = lambda a: a[:, :, ::-1, :]
    pairs = jnp.stack([jnp.concatenate([g[:, 0], rev(g[:, 1])], axis=-1),
                       jnp.concatenate([g[:, 2], rev(g[:, 3])], axis=-1)], axis=1)
    pairs = jnp.pad(pairs, ((0, 0), (0, SUBLANES - 2), (0, 0), (0, 0), (0, 0)))
    return pairs.transpose(2, 3, 0, 1, 4).reshape(batch * nc, DN_HEADS * SUBLANES, 2 * CHUNK)


def _trunk(x, mem, w, tm_inproj, final):
    batch, seq, d = x.shape
    t = batch * seq
    dd = DN_HEADS * DN_HEAD_DIM
    x0 = x.reshape(t, d)
    x1 = _ffn(x0, *w["ffn1"], w["final"], final=False)
    p, gt = _inproj(x1, w["mix"], w["w_main"], w["wgt"], w["alog"], w["dt"], tm=tm_inproj)
    gp = _gate_pairs(gt, batch, seq)
    o_dn = _deltanet(p.reshape(batch, seq, -1), w["conv"], gp, w["hn"], batch=batch, seq=seq)
    dfn = p.shape[1] - 4 * dd
    scale = (seq * (dfn // FN_GROUPS)) ** -0.5
    wch = (jnp.concatenate(w["chan"], axis=1) * scale).astype(BF16)
    rc, rs = _chandft(p, wch, col_block=4 * dd // dfn)
    ctw, stw = _twiddle(seq, min(SEQDFT_ROWS, seq // 2))
    o_fn = _seqdft(ctw, stw, rc, rs, batch=batch, seq=seq)
    x2 = _outproj(x1, o_dn.reshape(t, dd), o_fn.reshape(t, dfn), w["wout_a"], w["wout_b"])
    kv = _kvproj(mem, w["mn"], w["wkv"])
    x3 = _xattn(x2, w["xn"], w["wq"], kv, w["wo"], batch=batch, seq=seq)
    y = _ffn(x3.reshape(t, d), *w["ffn2"], w["final"], final=final)
    return y.reshape(batch, seq, d)


def kernel(x_prompt, x_sample, mem_prompt, mem_sample, ffn1_norm, ffn1_w_gate, ffn1_w_up, ffn1_w_down, mix_norm,
           w_in, conv_w, a_log, dt_bias, dn_head_norm, w_out, xattn_norm, mem_norm, xattn_w_q, xattn_w_kv,
           xattn_w_o, ffn2_norm, ffn2_w_gate, ffn2_w_up, ffn2_w_down, final_norm):
    depth = ffn1_norm.shape[0]
    tm_inproj = 512
    y_prompt, y_sample = x_prompt, x_sample
    for l in range(depth):
        w = _prep_weights(ffn1_norm[l], ffn1_w_gate[l], ffn1_w_up[l], ffn1_w_down[l], mix_norm[l], w_in[l],
                          conv_w[l], a_log[l], dt_bias[l], dn_head_norm[l], w_out[l], xattn_norm[l], mem_norm[l],
                          xattn_w_q[l], xattn_w_kv[l], xattn_w_o[l], ffn2_norm[l], ffn2_w_gate[l], ffn2_w_up[l],
                          ffn2_w_down[l], final_norm, tm_inproj=tm_inproj)
        y_prompt = _trunk(y_prompt, mem_prompt, w, tm_inproj, l == depth - 1)
        y_sample = _trunk(y_sample, mem_sample, w, tm_inproj, l == depth - 1)
    return (y_prompt, y_sample)
```

```python
import functools
import math

import jax
import jax.numpy as jnp
from jax import lax
from jax.experimental import pallas as pl
from jax.experimental.pallas import tpu as pltpu

F32 = jnp.float32
BF16 = jnp.bfloat16
EPS = 1e-6

DN_HEADS = 8
DN_HEAD_DIM = 128
FN_GROUPS = 4
CONV_WIDTH = 5
CHUNK = 64
XA_HEADS = 4
N_MEM = 256

LANES = 128
SUBLANES = 8
VMEM_LIMIT = 56 * 1024 * 1024

NT_DIMS = (((1,), (1,)), ((), ()))
TN_DIMS = (((0,), (0,)), ((), ()))


def _params(*sem):
    return pltpu.CompilerParams(dimension_semantics=sem, vmem_limit_bytes=VMEM_LIMIT)


def _dot(a, b):
    return jnp.dot(a, b, preferred_element_type=F32)


def _rms(x, g):
    ms = jnp.mean(x * x, axis=-1, keepdims=True)
    return x * lax.rsqrt(ms + EPS) * g


def _sigmoid(x):
    return 1.0 / (1.0 + jnp.exp(-x))


def _ffn_body(x_ref, g_ref, wg_ref, wu_ref, wd_ref, fin_ref, o_ref, h_ref, *, final, last_width):
    j = pl.program_id(1)
    nj = pl.num_programs(1)
    tf = wg_ref.shape[1]

    @pl.when(j == 0)
    def _():
        h_ref[...] = _rms(x_ref[...], g_ref[...]).astype(BF16)
        o_ref[...] = jnp.zeros_like(o_ref)

    def hidden_block(width):
        h = h_ref[...]
        a = _dot(h, wg_ref[:, 0:width])
        u = _dot(h, wu_ref[:, 0:width])
        act = (a * _sigmoid(a) * u).astype(BF16)
        o_ref[...] += _dot(act, wd_ref[0:width, :])

    if last_width == tf:
        hidden_block(tf)
    else:
        pl.when(j < nj - 1)(lambda: hidden_block(tf))
        pl.when(j == nj - 1)(lambda: hidden_block(last_width))

    @pl.when(j == nj - 1)
    def _():
        y = x_ref[...] + 0.5 * o_ref[...]
        if final:
            y = _rms(y, fin_ref[...])
        o_ref[...] = y


def _ffn(x, g, wg, wu, wd, fin, *, final, tm=1024, tf=512):
    t, d = x.shape
    tm = min(tm, t)
    assert t % tm == 0, (t, tm)
    dff = wg.shape[1]
    nj = pl.cdiv(dff, tf)
    last_width = dff - (nj - 1) * tf
    assert last_width % LANES == 0, (dff, tf)
    return pl.pallas_call(
        functools.partial(_ffn_body, final=final, last_width=last_width),
        grid=(t // tm, nj),
        in_specs=[
            pl.BlockSpec((tm, d), lambda i, j: (i, 0)),
            pl.BlockSpec((1, d), lambda i, j: (0, 0)),
            pl.BlockSpec((d, tf), lambda i, j: (0, j)),
            pl.BlockSpec((d, tf), lambda i, j: (0, j)),
            pl.BlockSpec((tf, d), lambda i, j: (j, 0)),
            pl.BlockSpec((1, d), lambda i, j: (0, 0)),
        ],
        out_specs=pl.BlockSpec((tm, d), lambda i, j: (i, 0)),
        out_shape=jax.ShapeDtypeStruct((t, d), F32),
        scratch_shapes=[pltpu.VMEM((tm, d), BF16)],
        compiler_params=_params("parallel", "arbitrary"),
        name="ffn_final" if final else "ffn",
    )(x, g, wg, wu, wd, fin)


def _inproj_body(x_ref, g_ref, w_ref, wgt_ref, alog_ref, dt_ref, p_ref, gt_ref, h_ref):
    j = pl.program_id(1)

    @pl.when(j == 0)
    def _():
        h = _rms(x_ref[...], g_ref[...]).astype(BF16)
        h_ref[...] = h
        gt = lax.dot_general(wgt_ref[...], h, NT_DIMS, preferred_element_type=F32)
        kind = lax.broadcasted_iota(jnp.int32, gt.shape, 0) & (SUBLANES - 1)
        xs = gt + dt_ref[...]
        softplus = jnp.maximum(xs, 0.0) + jnp.log1p(jnp.exp(-jnp.abs(xs)))
        log_decay = -jnp.exp(alog_ref[...]) * softplus
        gt_ref[...] = jnp.where(kind < 2, _sigmoid(gt), jnp.where(kind < 4, log_decay, 0.0))

    p_ref[...] = _dot(h_ref[...], w_ref[...]).astype(BF16)


def _inproj(x, g, w, wgt, alog, dt, *, tm, tn=2560):
    t, d = x.shape
    assert t % tm == 0, (t, tm)
    n = w.shape[1]
    rows = wgt.shape[0]
    return pl.pallas_call(
        _inproj_body,
        grid=(t // tm, n // tn),
        in_specs=[
            pl.BlockSpec((tm, d), lambda i, j: (i, 0)),
            pl.BlockSpec((1, d), lambda i, j: (0, 0)),
            pl.BlockSpec((d, tn), lambda i, j: (0, j)),
            pl.BlockSpec((rows, d), lambda i, j: (0, 0)),
            pl.BlockSpec((rows, tm), lambda i, j: (0, 0)),
            pl.BlockSpec((rows, tm), lambda i, j: (0, 0)),
        ],
        out_specs=[
            pl.BlockSpec((tm, tn), lambda i, j: (i, j)),
            pl.BlockSpec((rows, tm), lambda i, j: (0, i)),
        ],
        out_shape=[jax.ShapeDtypeStruct((t, n), BF16), jax.ShapeDtypeStruct((rows, t), F32)],
        scratch_shapes=[pltpu.VMEM((tm, d), BF16)],
        compiler_params=_params("parallel", "arbitrary"),
        name="inproj",
    )(x, g, w, wgt, alog, dt)


CONV_ROWS = 512
CONV_PIECE = 128
HALO = SUBLANES
PAIR = 2 * CHUNK


def _split2(x):
    hi = x.astype(BF16)
    lo = (x - hi.astype(F32)).astype(BF16)
    return hi, lo


def _aligned(x, m):
    return x if isinstance(x, int) else pl.multiple_of(x, m)


def _advance(gen):
    try:
        next(gen)
        return True
    except StopIteration:
        return False


PREP_STAGES = 14


def _deltanet_body(q_ref, k_ref, v_ref, z_ref, cwq_ref, cwk_ref, cwv_ref, gp_ref, hn_ref, o_ref,
                   upq, upk, upv, qs, ks, vs, pq_f, pq_b, n_f, n_b, oc_f, oc_b, el_scr, of, ob, bd1,
                   *, seq, unroll):
    nc = seq // CHUNK
    c = CHUNK
    nblk = nc // unroll

    streams = ((q_ref, cwq_ref, upq, qs, DN_HEAD_DIM ** -0.5), (k_ref, cwk_ref, upk, ks, 1.0),
               (v_ref, cwv_ref, upv, vs, None))
    for src_ref, _, up, _, _ in streams:
        zeros = jnp.zeros((HALO, LANES), F32)
        up[0:HALO, :] = zeros
        up[seq + HALO:seq + 2 * HALO, :] = zeros

        def fill(r, _, src_ref=src_ref, up=up):
            r0 = pl.multiple_of(r * CONV_ROWS, CONV_ROWS)
            up[pl.ds(r0 + HALO, CONV_ROWS), :] = src_ref[pl.ds(r0, CONV_ROWS), :].astype(F32)
            return 0

        lax.fori_loop(0, seq // CONV_ROWS, fill, 0)

    def conv_block(r0, cw_ref, up, dst_ref, scale):
        cw = cw_ref[...]
        for s0 in range(r0, r0 + CONV_ROWS, CONV_PIECE):
            y = None
            for tap in range(CONV_WIDTH):
                term = cw[tap:tap + 1, :] * up[pl.ds(s0 + (HALO + tap - CONV_WIDTH // 2), CONV_PIECE), :]
                y = term if y is None else y + term
            y = y / (1.0 + jnp.exp(-y))
            if scale is not None:
                y = y * lax.rsqrt(jnp.sum(y * y, axis=-1, keepdims=True) + EPS)
                if scale != 1.0:
                    y = y * scale
            dst_ref[pl.ds(s0, CONV_PIECE), :] = y

    def conv_rows(rows):
        for r0 in rows:
            for _, cw_ref, up, dst_ref, scale in streams:
                conv_block(r0, cw_ref, up, dst_ref, scale)
                yield

    edge = unroll * c
    all_rows = range(0, seq, CONV_ROWS)
    edge_rows = [r for r in all_rows if r < edge or r + CONV_ROWS > seq - edge]
    mid_rows = [r for r in all_rows if r not in edge_rows]
    for _ in conv_rows(edge_rows):
        pass

    ii = lax.broadcasted_iota(jnp.int32, (c, PAIR), 0)
    ll = lax.broadcasted_iota(jnp.int32, (c, PAIR), 1)
    jj = ll & (c - 1)
    is_f = ll < c
    ahead = jnp.where(is_f, ii - jj, jj - ii)
    incl = ahead >= 0
    strict = ahead > 0
    eye = ii == jj
    xor = ii ^ jj

    rr = lax.broadcasted_iota(jnp.int32, (2 * PAIR, 2 * PAIR), 0)
    nn = lax.broadcasted_iota(jnp.int32, (2 * PAIR, 2 * PAIR), 1)
    bd1[...] = jnp.where(((rr & (PAIR - 1)) >> (c.bit_length() - 1)) == (nn >> (PAIR.bit_length() - 1)),
                         1.0, 0.0).astype(BF16)

    def blockdiag(xp):
        top = jnp.where(is_f, xp, 0.0)
        bot = jnp.where(is_f, 0.0, xp)
        return jnp.concatenate([top, bot], axis=0).astype(BF16)

    is_b = ll >= c
    zer = jnp.zeros((c, LANES), BF16)

    def prep_pair(i, slot):
        st = []
        for i, slot in ((i, slot),):
            ra = _aligned(i * c, c)
            rb = _aligned((nc - 1 - i) * c, c)
            s = dict(slot=slot, ka=ks[pl.ds(ra, c), :], qa=qs[pl.ds(ra, c), :],
                     kz=ks[pl.ds(rb, c), :], qz=qs[pl.ds(rb, c), :])
            ka16, qa16, kz16, qz16 = (s[n].astype(BF16) for n in ("ka", "qa", "kz", "qz"))
            va16 = vs[pl.ds(ra, c), :].astype(BF16)
            vz16 = vs[pl.ds(rb, c), :].astype(BF16)
            s["bdv"] = jnp.concatenate([jnp.concatenate([va16, zer], axis=1),
                                        jnp.concatenate([zer, vz16], axis=1)], axis=0)
            s["bdk"] = jnp.concatenate([jnp.concatenate([ka16, zer], axis=1),
                                        jnp.concatenate([zer, kz16], axis=1)], axis=0)
            s["gram"] = lax.dot_general(
                jnp.concatenate([jnp.concatenate([ka16, qa16], axis=0), jnp.concatenate([kz16, qz16], axis=0)], axis=1),
                s["bdk"], NT_DIMS, preferred_element_type=F32)
            tile = gp_ref[i]
            s["brow"] = tile[0:1, :]
            grow = tile[1:2, :]
            x = jnp.concatenate([jnp.where(incl, grow, 0.0), jnp.where(eye, s["brow"], 0.0)], axis=0)
            hi, lo = _split2(x)
            s["cb"] = _dot(jnp.concatenate([hi, lo], axis=1), bd1[...])
            st.append(s)
        yield
        for s in st:
            cb, gram = s["cb"], s["gram"]
            s["cf_b"] = cb[0:c, 0:PAIR]
            s["cb_b"] = cb[0:c, PAIR:2 * PAIR]
            cpair = jnp.where(is_f, s["cf_b"], s["cb_b"])
            bpair = jnp.where(is_f, cb[c:2 * c, 0:PAIR], cb[c:2 * c, PAIR:2 * PAIR])
            s["crow"] = jnp.sum(jnp.where(eye, cpair, 0.0), axis=0, keepdims=True)
            decay = jnp.where(incl, jnp.exp(jnp.where(incl, cpair - s["crow"], 0.0)), 0.0)
            s["a"] = jnp.where(strict, bpair * gram[0:c, :] * decay, 0.0)
            s["attn16"] = (gram[c:2 * c, :] * decay).astype(BF16)

        ts = [jnp.where(eye, 1.0, 0.0) - jnp.where(xor == 1, s["a"], 0.0) for s in st]
        for lvl in range(1, 6):
            es = [jnp.where((xor >> lvl) == 1, s["a"], 0.0) for s in st]
            te = _dot(ts[0].astype(BF16), blockdiag(es[0]))
            yield
            tet = _dot(te.astype(BF16), blockdiag(ts[0]))
            yield
            ts = [t - tet[:, n * PAIR:(n + 1) * PAIR] for n, t in enumerate(ts)]

        for s, t in zip(st, ts):
            erow = jnp.exp(s["crow"])
            s["u"] = _dot((t * s["brow"]).astype(BF16), s["bdv"])
            s["w"] = _dot((t * (s["brow"] * erow)).astype(BF16), s["bdk"])
        yield

        for s in st:
            s["last_f"] = s["cf_b"][c - 1:c, :]
            s["last_b"] = s["cb_b"][0:1, :]
            s["dirs"] = ((0, s["ka"], s["qa"], s["cf_b"], s["last_f"], is_f, pq_f, n_f, oc_f),
                         (LANES, s["kz"], s["qz"], s["cb_b"], s["last_b"], is_b, pq_b, n_b, oc_b))
            s["prods"] = []
            for (lo, kx, qx, cx, last, keep, pq, nn_ref, oc) in s["dirs"]:
                uw = jnp.concatenate([s["u"][:, lo:lo + LANES], s["w"][:, lo:lo + LANES]], axis=1).astype(BF16)
                kdec = (kx * jnp.exp(last - cx)).astype(BF16)
                n_p = lax.dot_general(kdec, uw, TN_DIMS, preferred_element_type=F32)
                o_aw = _dot(jnp.where(keep, s["attn16"], zer), jnp.concatenate([uw, uw], axis=0))
                s["prods"].append((n_p, o_aw))
        yield
        for s in st:
            slot = s["slot"]
            p0 = _aligned(slot * (3 * c), c)
            n0 = _aligned(slot * (2 * c), 2 * c)
            o0 = _aligned(slot * c, c)
            for (lo, kx, qx, cx, last, keep, pq, nn_ref, oc), (n_p, o_aw) in zip(s["dirs"], s["prods"]):
                pq[pl.ds(p0, 2 * c), :] = n_p[:, LANES:2 * LANES].astype(BF16)
                pq[pl.ds(p0 + 2 * c, c), :] = (qx * jnp.exp(cx) - o_aw[:, LANES:2 * LANES]).astype(BF16)
                nn_ref[pl.ds(n0, 2 * c), :] = n_p[:, 0:LANES]
                oc[pl.ds(o0, c), :] = o_aw[:, 0:LANES]
            el_scr[slot] = jnp.concatenate([jnp.exp(s["last_f"]), jnp.exp(s["last_b"]),
                                            jnp.zeros((SUBLANES - 2, LANES), F32)], axis=0)

    def rec_block(blk, state):
        s_f, s_b = state
        for j in range(unroll):
            i = blk * unroll + j
            slot = (blk & 1) * unroll + j
            p0 = _aligned(slot * (3 * c), c)
            n0 = _aligned(slot * (2 * c), 2 * c)
            o0 = _aligned(slot * c, c)
            el = el_scr[slot]
            r_f = _dot(pq_f[pl.ds(p0, 3 * c), :], s_f.astype(BF16))
            r_b = _dot(pq_b[pl.ds(p0, 3 * c), :], s_b.astype(BF16))
            yield
            of[pl.ds(_aligned(i * c, c), c), :] = oc_f[pl.ds(o0, c), :] + r_f[2 * c:3 * c, :]
            s_f = s_f * el[0:1, :] + n_f[pl.ds(n0, 2 * c), :] - r_f[0:2 * c, :]
            ob[pl.ds(_aligned((nc - 1 - i) * c, c), c), :] = oc_b[pl.ds(o0, c), :] + r_b[2 * c:3 * c, :]
            s_b = s_b * el[1:2, :] + n_b[pl.ds(n0, 2 * c), :] - r_b[0:2 * c, :]
        state[0], state[1] = s_f, s_b

    def run_interleaved(gens, rec=None):
        rec_period = max(1, (PREP_STAGES * len(gens)) // (unroll + 1))
        steps = 0
        while gens or rec is not None:
            alive = []
            for g in gens:
                if _advance(g):
                    alive.append(g)
                steps += 1
                if rec is not None and steps % rec_period == 0 and not _advance(rec):
                    rec = None
            gens = alive
            if not gens and rec is not None and not _advance(rec):
                rec = None

    def prep_gens(blk):
        return [prep_pair(blk * unroll + j, (blk & 1) * unroll + j) for j in range(unroll)]

    def pipelined(blk, carry):
        state = list(carry)
        run_interleaved(prep_gens(blk), rec_block(blk - 1, state))
        return tuple(state)

    def finish_rows(rows):
        hn = hn_ref[...]
        for r0 in rows:
            for s0 in range(r0, r0 + CONV_ROWS, CONV_PIECE):
                o = of[pl.ds(s0, CONV_PIECE), :] + ob[pl.ds(s0, CONV_PIECE), :]
                zz = z_ref[pl.ds(s0, CONV_PIECE), :].astype(F32)
                o_ref[pl.ds(s0, CONV_PIECE), :] = (_rms(o, hn) * (zz / (1.0 + jnp.exp(-zz)))).astype(BF16)
            yield

    run_interleaved(prep_gens(0) + [conv_rows(mid_rows)])
    s0 = jnp.zeros((LANES, LANES), F32)
    state = list(lax.fori_loop(1, nblk, pipelined, (s0, s0)))
    run_interleaved([finish_rows(mid_rows)] if nblk > 1 else [], rec_block(nblk - 1, state))
    for _ in finish_rows(edge_rows if nblk > 1 else all_rows):
        pass


DN_UNROLL = 16
DN_MIN_BLOCKS = 4


def _deltanet(p, cw, gp, hn, *, batch, seq):
    assert seq % CONV_ROWS == 0 and CONV_ROWS % CHUNK == 0, (seq, CONV_ROWS)
    nc = seq // CHUNK
    unroll = math.gcd(nc, DN_UNROLL)
    while unroll > 1 and nc // unroll < DN_MIN_BLOCKS:
        unroll //= 2
    h = DN_HEADS
    slots = 2 * unroll
    c = CHUNK
    col = lambda off: pl.BlockSpec((None, seq, LANES), lambda b, hd, off=off: (b, 0, off + hd))
    cws = lambda off: pl.BlockSpec((SUBLANES, LANES), lambda b, hd, off=off: (0, off + hd))
    return pl.pallas_call(
        functools.partial(_deltanet_body, seq=seq, unroll=unroll),
        grid=(batch, h),
        in_specs=[col(0), col(h), col(2 * h), col(3 * h), cws(0), cws(h), cws(2 * h),
                  pl.BlockSpec((nc, SUBLANES, LANES), lambda b, hd: (b, hd, 0)),
                  pl.BlockSpec((1, LANES), lambda b, hd: (0, 0))],
        out_specs=pl.BlockSpec((None, seq, LANES), lambda b, hd: (b, 0, hd)),
        out_shape=jax.ShapeDtypeStruct((batch, seq, h * LANES), BF16),
        scratch_shapes=[
            pltpu.VMEM((seq + 2 * HALO, LANES), F32),
            pltpu.VMEM((seq + 2 * HALO, LANES), F32),
            pltpu.VMEM((seq + 2 * HALO, LANES), F32),
            pltpu.VMEM((seq, LANES), F32),
            pltpu.VMEM((seq, LANES), F32),
            pltpu.VMEM((seq, LANES), F32),
            pltpu.VMEM((slots * 3 * c, LANES), BF16),
            pltpu.VMEM((slots * 3 * c, LANES), BF16),
            pltpu.VMEM((slots * 2 * c, LANES), F32),
            pltpu.VMEM((slots * 2 * c, LANES), F32),
            pltpu.VMEM((slots * c, LANES), F32),
            pltpu.VMEM((slots * c, LANES), F32),
            pltpu.VMEM((slots, SUBLANES, LANES), F32),
            pltpu.VMEM((seq, LANES), F32),
            pltpu.VMEM((seq, LANES), F32),
            pltpu.VMEM((2 * PAIR, 2 * PAIR), BF16),
        ],
        compiler_params=_params("parallel", "parallel"),
        name="deltanet",
    )(p, p, p, p, cw, cw, cw, gp, hn)


TW_ROWS = 64


def _twiddle_body(ac_ref, as_ref, bc_ref, bs_ref, c_ref, s_ref):
    ac = ac_ref[...]
    sn = as_ref[...]
    bc = bc_ref[...]
    bs = bs_ref[...]
    c_ref[...] = (ac * bc - sn * bs).astype(BF16)
    s_ref[...] = (-(sn * bc + ac * bs)).astype(BF16)


def _twiddle(seq, tm):
    nblk = seq // 2 // tm
    rows = tm + SUBLANES
    kk = jnp.arange(seq, dtype=jnp.int32)[None, :]
    step = 2.0 * math.pi / seq
    ang_a = ((jnp.arange(rows, dtype=jnp.int32)[:, None] * kk) % seq).astype(F32) * step
    ang_b = ((jnp.arange(nblk, dtype=jnp.int32)[:, None] * tm * kk) % seq).astype(F32) * step
    tab = pl.BlockSpec((rows, seq), lambda i: (0, 0))
    row = pl.BlockSpec((None, 1, seq), lambda i: (i, 0, 0))
    out = pl.BlockSpec((None, rows, seq), lambda i: (i, 0, 0))
    return pl.pallas_call(
        _twiddle_body,
        grid=(nblk,),
        in_specs=[tab, tab, row, row],
        out_specs=[out, out],
        out_shape=[jax.ShapeDtypeStruct((nblk, rows, seq), BF16)] * 2,
        compiler_params=_params("parallel"),
        name="twiddle",
    )(jnp.cos(ang_a), jnp.sin(ang_a), jnp.cos(ang_b)[:, None, :], jnp.sin(ang_b)[:, None, :])


def _chandft_body(u_ref, w_ref, rc_ref, rs_ref):
    gd = w_ref.shape[0]
    w = w_ref[...]
    for g in range(FN_GROUPS):
        y = _dot(u_ref[:, g * gd:(g + 1) * gd], w)
        rc_ref[:, g * gd:(g + 1) * gd] = y[:, 0:gd].astype(BF16)
        rs_ref[:, g * gd:(g + 1) * gd] = y[:, gd:2 * gd].astype(BF16)


def _chandft(p, wch, *, col_block, tm=512):
    t = p.shape[0]
    dfn = FN_GROUPS * wch.shape[0]
    out = pl.BlockSpec((tm, dfn), lambda i: (i, 0))
    return pl.pallas_call(
        _chandft_body,
        grid=(t // tm,),
        in_specs=[pl.BlockSpec((tm, dfn), lambda i: (i, col_block)),
                  pl.BlockSpec(wch.shape, lambda i: (0, 0))],
        out_specs=[out, out],
        out_shape=[jax.ShapeDtypeStruct((t, dfn), BF16)] * 2,
        compiler_params=_params("parallel"),
        name="chandft",
    )(p, wch)


def _seqdft_body(c_ref, s_ref, rc_ref, rs_ref, o_ref, hi_scr):
    tm = o_ref.shape[0]
    rows = c_ref.shape[0]

    @pl.when(pl.program_id(2) == 0)
    def _():
        p = _dot(c_ref[...], rc_ref[...])
        mq = _dot(s_ref[...], rs_ref[...])
        o_ref[...] = (p[0:tm, :] + mq[0:tm, :]).astype(BF16)
        jj = lax.broadcasted_iota(jnp.int32, (tm, rows), 0)
        mm = lax.broadcasted_iota(jnp.int32, (tm, rows), 1)
        pick = jnp.where(jj + mm == tm, 1.0, 0.0).astype(BF16)
        hi_scr[...] = _dot(pick, (p - mq).astype(BF16)).astype(BF16)

    @pl.when(pl.program_id(2) == 1)
    def _():
        o_ref[...] = hi_scr[...]


SEQDFT_ROWS = 256


def _seqdft(ctw, stw, rc, rs, *, batch, seq):
    dfn = rc.shape[-1]
    nblk, rows, _ = ctw.shape
    tm = rows - SUBLANES
    tw = pl.BlockSpec((None, rows, seq), lambda b, i, h: (jnp.minimum(i + h, nblk - 1), 0, 0))
    rhs = pl.BlockSpec((None, seq, dfn), lambda b, i, h: (b, 0, 0))
    return pl.pallas_call(
        _seqdft_body,
        grid=(batch, nblk, 2),
        in_specs=[tw, tw, rhs, rhs],
        out_specs=pl.BlockSpec((None, tm, dfn),
                               lambda b, i, h: (b, i + h * (2 * nblk - 1 - 2 * i), 0)),
        out_shape=jax.ShapeDtypeStruct((batch, seq, dfn), BF16),
        scratch_shapes=[pltpu.VMEM((tm, dfn), BF16)],
        compiler_params=_params("parallel", "arbitrary", "arbitrary"),
        name="seqdft",
    )(ctw, stw, rc.reshape(batch, seq, dfn), rs.reshape(batch, seq, dfn))


def _outproj_body(x_ref, a_ref, b_ref, wa_ref, wb_ref, o_ref):
    o_ref[...] = x_ref[...] + _dot(a_ref[...], wa_ref[...]) + _dot(b_ref[...], wb_ref[...])


def _outproj(x, a, b, wa, wb, *, tm=512):
    t, d = x.shape
    da = a.shape[1]
    db = b.shape[1]
    return pl.pallas_call(
        _outproj_body,
        grid=(t // tm,),
        in_specs=[pl.BlockSpec((tm, d), lambda i: (i, 0)),
                  pl.BlockSpec((tm, da), lambda i: (i, 0)),
                  pl.BlockSpec((tm, db), lambda i: (i, 0)),
                  pl.BlockSpec((da, d), lambda i: (0, 0)),
                  pl.BlockSpec((db, d), lambda i: (0, 0))],
        out_specs=pl.BlockSpec((tm, d), lambda i: (i, 0)),
        out_shape=jax.ShapeDtypeStruct((t, d), F32),
        compiler_params=_params("parallel"),
        name="outproj",
    )(x, a, b, wa, wb)


def _kvproj_body(m_ref, g_ref, w_ref, o_ref, h_ref):
    @pl.when(pl.program_id(1) == 0)
    def _():
        h_ref[...] = _rms(m_ref[...], g_ref[...]).astype(BF16)

    o_ref[...] = _dot(h_ref[...], w_ref[...]).astype(BF16)


def _kvproj(mem, g, w, *, tn=1024):
    b, n, d = mem.shape
    nout = w.shape[1]
    return pl.pallas_call(
        _kvproj_body,
        grid=(b, nout // tn),
        in_specs=[pl.BlockSpec((None, n, d), lambda i, j: (i, 0, 0)),
                  pl.BlockSpec((1, d), lambda i, j: (0, 0)),
                  pl.BlockSpec((d, tn), lambda i, j: (0, j))],
        out_specs=pl.BlockSpec((None, n, tn), lambda i, j: (i, 0, j)),
        out_shape=jax.ShapeDtypeStruct((b, n, nout), BF16),
        scratch_shapes=[pltpu.VMEM((n, d), BF16)],
        compiler_params=_params("parallel", "arbitrary"),
        name="kvproj",
    )(mem, g, w)


def _xattn_body(x_ref, g_ref, wq_ref, kv_ref, wo_ref, o_ref, q_scr, a_scr):
    d = x_ref.shape[-1]
    hd = d // XA_HEADS
    x = x_ref[...]
    q_scr[...] = _dot(_rms(x, g_ref[...]).astype(BF16), wq_ref[...]).astype(BF16)
    for h in range(XA_HEADS):
        kh = kv_ref[:, h * hd:(h + 1) * hd]
        vh = kv_ref[:, d + h * hd:d + (h + 1) * hd]
        s = lax.dot_general(q_scr[:, h * hd:(h + 1) * hd], kh, NT_DIMS,
                            preferred_element_type=F32) * (hd ** -0.5)
        p = jnp.exp(s - jnp.max(s, axis=-1, keepdims=True))
        l = jnp.sum(p, axis=-1, keepdims=True)
        a_scr[:, h * hd:(h + 1) * hd] = (_dot(p.astype(BF16), vh) / l).astype(BF16)
    o_ref[...] = x + _dot(a_scr[...], wo_ref[...])


def _xattn(x, g, wq, kv, wo, *, batch, seq, tm=512):
    d = x.shape[-1]
    n = kv.shape[1]
    tm = min(tm, seq)
    resident = lambda: pl.BlockSpec((d, d), lambda b, i: (0, 0), pipeline_mode=pl.Buffered(1))
    return pl.pallas_call(
        _xattn_body,
        grid=(batch, seq // tm),
        in_specs=[pl.BlockSpec((None, tm, d), lambda b, i: (b, i, 0)),
                  pl.BlockSpec((1, d), lambda b, i: (0, 0)),
                  resident(),
                  pl.BlockSpec((None, n, 2 * d), lambda b, i: (b, 0, 0)),
                  resident()],
        out_specs=pl.BlockSpec((None, tm, d), lambda b, i: (b, i, 0)),
        out_shape=jax.ShapeDtypeStruct((batch, seq, d), F32),
        scratch_shapes=[pltpu.VMEM((tm, d), BF16), pltpu.VMEM((tm, d), BF16)],
        compiler_params=_params("parallel", "arbitrary"),
        name="xattn",
    )(x.reshape(batch, seq, d), g, wq, kv, wo)


def _prep_weights(ffn1_norm, ffn1_w_gate, ffn1_w_up, ffn1_w_down, mix_norm, w_in, conv_w, a_log, dt_bias,
                  dn_head_norm, w_out, xattn_norm, mem_norm, xattn_w_q, xattn_w_kv, xattn_w_o, ffn2_norm,
                  ffn2_w_gate, ffn2_w_up, ffn2_w_down, final_norm, *, tm_inproj):
    d = w_in.shape[0]
    dd = DN_HEADS * DN_HEAD_DIM

    def ffn_w(wg, wu, wd):
        return (wg.astype(BF16), wu.astype(BF16), wd.astype(BF16))

    row = lambda v: v.reshape(1, -1).astype(F32)
    off = 4 * dd
    w_main = jnp.concatenate([w_in[:, :off], w_in[:, off + 4 * DN_HEADS:]], axis=1).astype(BF16)
    beta_w = w_in[:, off:off + 2 * DN_HEADS].reshape(d, 2, DN_HEADS)
    a_w = w_in[:, off + 2 * DN_HEADS:off + 4 * DN_HEADS].reshape(d, 2, DN_HEADS)
    gate_w = jnp.concatenate([beta_w, a_w, jnp.zeros((d, SUBLANES - 4, DN_HEADS), F32)], axis=1)
    wgt = gate_w.transpose(2, 1, 0).reshape(DN_HEADS * SUBLANES, d).astype(BF16)

    def gate_param(v):
        full = jnp.concatenate([jnp.zeros((2, DN_HEADS), F32), v.astype(F32),
                                jnp.zeros((SUBLANES - 4, DN_HEADS), F32)], axis=0)
        return jnp.broadcast_to(full.T.reshape(DN_HEADS * SUBLANES, 1), (DN_HEADS * SUBLANES, tm_inproj))

    gd = (w_in.shape[1] - off - 4 * DN_HEADS) // FN_GROUPS
    idx = jnp.arange(gd, dtype=jnp.int32)
    ang = ((idx[:, None] * idx[None, :]) % gd).astype(F32) * (2.0 * math.pi / gd)
    return dict(
        ffn1=(row(ffn1_norm),) + ffn_w(ffn1_w_gate, ffn1_w_up, ffn1_w_down),
        ffn2=(row(ffn2_norm),) + ffn_w(ffn2_w_gate, ffn2_w_up, ffn2_w_down),
        final=row(final_norm), mix=row(mix_norm), w_main=w_main, wgt=wgt,
        alog=gate_param(a_log), dt=gate_param(dt_bias),
        conv=jnp.pad(conv_w.astype(F32), ((0, SUBLANES - CONV_WIDTH), (0, 0))),
        hn=row(dn_head_norm), chan=(jnp.cos(ang), jnp.sin(ang)),
        wout_a=w_out[:dd].astype(BF16), wout_b=w_out[dd:].astype(BF16),
        xn=row(xattn_norm), mn=row(mem_norm), wq=xattn_w_q.astype(BF16), wkv=xattn_w_kv.astype(BF16),
        wo=xattn_w_o.astype(BF16))


def _gate_pairs(gt, batch, seq):
    nc = seq // CHUNK
    g = gt.reshape(DN_HEADS, SUBLANES, batch, nc, CHUNK)
    rev = lambda a: a[:, :, ::-1, :]
    pairs = jnp.stack([jnp.concatenate([g[:, 0], rev(g[:, 1])], axis=-1),
                       jnp.concatenate([g[:, 2], rev(g[:, 3])], axis=-1)], axis=1)
    pairs = jnp.pad(pairs, ((0, 0), (0, SUBLANES - 2), (0, 0), (0, 0), (0, 0)))
    return pairs.transpose(2, 3, 0, 1, 4).reshape(batch * nc, DN_HEADS * SUBLANES, 2 * CHUNK)


def _trunk(x, mem, w, tm_inproj, final):
    batch, seq, d = x.shape
    t = batch * seq
    dd = DN_HEADS * DN_HEAD_DIM
    x0 = x.reshape(t, d)
    x1 = _ffn(x0, *w["ffn1"], w["final"], final=False)
    p, gt = _inproj(x1, w["mix"], w["w_main"], w["wgt"], w["alog"], w["dt"], tm=tm_inproj)
    gp = _gate_pairs(gt, batch, seq)
    o_dn = _deltanet(p.reshape(batch, seq, -1), w["conv"], gp, w["hn"], batch=batch, seq=seq)
    dfn = p.shape[1] - 4 * dd
    scale = (seq * (dfn // FN_GROUPS)) ** -0.5
    wch = (jnp.concatenate(w["chan"], axis=1) * scale).astype(BF16)
    rc, rs = _chandft(p, wch, col_block=4 * dd // dfn)
    ctw, stw = _twiddle(seq, min(SEQDFT_ROWS, seq // 2))
    o_fn = _seqdft(ctw, stw, rc, rs, batch=batch, seq=seq)
    x2 = _outproj(x1, o_dn.reshape(t, dd), o_fn.reshape(t, dfn), w["wout_a"], w["wout_b"])
    kv = _kvproj(mem, w["mn"], w["wkv"])
    x3 = _xattn(x2, w["xn"], w["wq"], kv, w["wo"], batch=batch, seq=seq)
    y = _ffn(x3.reshape(t, d), *w["ffn2"], w["final"], final=final)
    return y.reshape(batch, seq, d)


def kernel(x_prompt, x_sample, mem_prompt, mem_sample, ffn1_norm, ffn1_w_gate, ffn1_w_up, ffn1_w_down, mix_norm,
           w_in, conv_w, a_log, dt_bias, dn_head_norm, w_out, xattn_norm, mem_norm, xattn_w_q, xattn_w_kv,
           xattn_w_o, ffn2_norm, ffn2_w_gate, ffn2_w_up, ffn2_w_down, final_norm):
    depth = ffn1_norm.shape[0]
    tm_inproj = math.gcd(512, x_prompt.shape[0] * x_prompt.shape[1], x_sample.shape[0] * x_sample.shape[1])
    y_prompt, y_sample = x_prompt, x_sample
    for l in range(depth):
        w = _prep_weights(ffn1_norm[l], ffn1_w_gate[l], ffn1_w_up[l], ffn1_w_down[l], mix_norm[l], w_in[l],
                          conv_w[l], a_log[l], dt_bias[l], dn_head_norm[l], w_out[l], xattn_norm[l], mem_norm[l],
                          xattn_w_q[l], xattn_w_kv[l], xattn_w_o[l], ffn2_norm[l], ffn2_w_gate[l], ffn2_w_up[l],
                          ffn2_w_down[l], final_norm, tm_inproj=tm_inproj)
        y_prompt = _trunk(y_prompt, mem_prompt, w, tm_inproj, l == depth - 1)
        y_sample = _trunk(y_sample, mem_sample, w, tm_inproj, l == depth - 1)
    return (y_prompt, y_sample)
```
